```python
import jax, jax.numpy as jnp
from jax import lax
import numpy as np

D_MODEL = 2048
BATCH = 2
SEQ = 16384
DEPTH = 1

D_A = 1024
CHUNK = 128
N_GROUPS_A = 8
GROUP_W_A = D_A // N_GROUPS_A
D_B = 1024
HEAD_B = 64
N_HEADS_B = D_B // HEAD_B
DECAY_LORA = 64
AAA_LORA = 64
GATE_LORA = 160
D_FF = 5632
CONV_W = 3
RMS_EPS = 1e-6
LN_EPS = 1e-5
GN_EPS = 64e-5
D_B_IN = 3 * D_B + DECAY_LORA + AAA_LORA + GATE_LORA
D_IN = 2 * D_A + D_B_IN + 2 * D_MODEL

kernel_name = 'hybrid_gmlp_rwkv7_gated_block'


def rmsnorm(x, g):
    x32 = x.astype(jnp.float32)
    y = x32 * lax.rsqrt(jnp.mean(x32 * x32, axis=-1, keepdims=True) + RMS_EPS)
    return (y * g.astype(jnp.float32)).astype(x.dtype)


def layernorm(x, w, b):
    x32 = x.astype(jnp.float32)
    mu = jnp.mean(x32, axis=-1, keepdims=True)
    var = jnp.mean(jnp.square(x32 - mu), axis=-1, keepdims=True)
    y = (x32 - mu) * lax.rsqrt(var + LN_EPS)
    return (y * w.astype(jnp.float32) + b.astype(jnp.float32)).astype(x.dtype)


def token_shift(x):
    return jnp.pad(x[:, :-1], ((0, 0), (1, 0), (0, 0)))


def causal_dwconv(x, w, b):
    S = x.shape[1]
    xp = jnp.pad(x, ((0, 0), (CONV_W - 1, 0), (0, 0)))
    y = b
    for j in range(CONV_W):
        y = y + w[j] * xp[:, j:j + S]
    return y


def gmlp_spatial_gating(u, v, ln_w, ln_b, ws, bs):
    Bsz, S, _ = v.shape
    n_chunks = S // CHUNK
    v = layernorm(v, ln_w, ln_b)
    vc = v.reshape(Bsz, n_chunks, CHUNK, N_GROUPS_A, GROUP_W_A)
    causal = jnp.tril(jnp.ones((CHUNK, CHUNK), dtype=bool))
    wm = jnp.where(causal[None], ws, jnp.zeros_like(ws))
    mixed = jnp.einsum('gts,bnsgc->bntgc', wm, vc) + bs.T[None, None, :, :, None]
    return u * mixed.reshape(Bsz, S, D_A)


def wkv7_scan(r, decay, k, v, a_vec, b_vec):
    Bsz = r.shape[0]

    def step(state, inp):
        r_t, w_t, k_t, v_t, a_t, b_t = inp
        sa = jnp.einsum('bhvk,bhk->bhv', state, a_t)
        state = (state * w_t[:, :, None, :]
                 + sa[..., None] * b_t[:, :, None, :]
                 + v_t[..., None] * k_t[:, :, None, :])
        y_t = jnp.einsum('bhvk,bhk->bhv', state, r_t)
        return state, y_t

    xs = tuple(jnp.moveaxis(t, 1, 0) for t in (r, decay, k, v, a_vec, b_vec))
    s0 = jnp.zeros((Bsz, N_HEADS_B, HEAD_B, HEAD_B), jnp.float32)
    _, ys = lax.scan(step, s0, xs)
    return jnp.moveaxis(ys, 0, 1)


def rwkv7_time_mix(feat, mu, w0, w2, a0, a2, g2, k_k, k_a, r_k, ln_w, ln_b):
    f32 = jnp.float32
    Bsz, S, _ = feat.shape
    feat = feat + (token_shift(feat) - feat) * mu
    r, k, v, wlo, alo, glo = jnp.split(
        feat, [D_B, 2 * D_B, 3 * D_B, 3 * D_B + DECAY_LORA, 3 * D_B + DECAY_LORA + AAA_LORA], axis=-1)
    w = -jax.nn.softplus(-(w0 + jnp.tanh(wlo) @ w2).astype(f32)) - 0.5
    decay = jnp.exp(-jnp.exp(w))
    a = jax.nn.sigmoid(a0 + alo @ a2)
    g = jax.nn.sigmoid(glo) @ g2

    def heads(t):
        return t.reshape(Bsz, S, N_HEADS_B, HEAD_B)

    kk = heads(k * k_k).astype(f32)
    kk = kk / jnp.maximum(jnp.sqrt(jnp.sum(kk * kk, axis=-1, keepdims=True)), 1e-12)
    k = k * (1.0 + (a - 1.0) * k_a)
    rh, kh, vh, ah = heads(r).astype(f32), heads(k).astype(f32), heads(v).astype(f32), heads(a).astype(f32)
    y = wkv7_scan(rh, heads(decay), kh, vh, -kk, kk * ah)
    mean = jnp.mean(y, axis=-1, keepdims=True)
    var = jnp.mean(jnp.square(y - mean), axis=-1, keepdims=True)
    y = ((y - mean) * lax.rsqrt(var + GN_EPS)).reshape(Bsz, S, D_B)
    y = y * ln_w.astype(f32) + ln_b.astype(f32)
    bonus = jnp.sum(rh * kh * r_k.astype(f32), axis=-1, keepdims=True) * vh
    y = y + bonus.reshape(Bsz, S, D_B)
    return (y * g.astype(f32)).astype(feat.dtype)


def setup_inputs(seed: int = 0) -> dict:
    key = jax.random.key(seed)
    ks = jax.random.split(key, 32)
    f32 = jnp.float32
    L = DEPTH

    def nrm(k, shape, scale):
        return jax.random.normal(k, shape, f32) * scale

    return {
        'x': nrm(ks[0], (BATCH, SEQ, D_MODEL), 1.0),
        'norm1_g': 1.0 + nrm(ks[1], (L, D_MODEL), 0.02),
        'w_in': nrm(ks[2], (L, D_MODEL, D_IN), D_MODEL ** -0.5),
        'mu_b': jax.random.uniform(ks[3], (L, D_B_IN), f32),
        'rwkv_w0': jax.random.uniform(ks[4], (L, D_B), f32, -5.0, 1.0),
        'rwkv_w2': nrm(ks[5], (L, DECAY_LORA, D_B), 0.1 * DECAY_LORA ** -0.5),
        'rwkv_a0': nrm(ks[6], (L, D_B), 0.5),
        'rwkv_a2': nrm(ks[7], (L, AAA_LORA, D_B), AAA_LORA ** -0.5),
        'rwkv_g2': nrm(ks[8], (L, GATE_LORA, D_B), GATE_LORA ** -0.5),
        'rwkv_kk': 0.85 + nrm(ks[9], (L, D_B), 0.05),
        'rwkv_ka': 1.0 + nrm(ks[10], (L, D_B), 0.05),
        'rwkv_rk': nrm(ks[11], (L, N_HEADS_B, HEAD_B), 0.1),
        'rwkv_ln_w': 1.0 + nrm(ks[12], (L, D_B), 0.02),
        'rwkv_ln_b': nrm(ks[13], (L, D_B), 0.02),
        'gmlp_ln_w': 1.0 + nrm(ks[14], (L, D_A), 0.02),
        'gmlp_ln_b': nrm(ks[15], (L, D_A), 0.02),
        'gmlp_ws': nrm(ks[16], (L, N_GROUPS_A, CHUNK, CHUNK), CHUNK ** -0.5),
        'gmlp_bs': 1.0 + nrm(ks[17], (L, N_GROUPS_A, CHUNK), 0.1),
        'w_proj_a': nrm(ks[18], (L, D_A, D_MODEL), D_A ** -0.5),
        'w_proj_b': nrm(ks[19], (L, D_B, D_MODEL), D_B ** -0.5),
        'w_out': nrm(ks[20], (L, D_MODEL, D_MODEL), D_MODEL ** -0.5),
        'norm2_g': 1.0 + nrm(ks[21], (L, D_MODEL), 0.02),
        'w_up': nrm(ks[22], (L, D_MODEL, 2 * D_FF), D_MODEL ** -0.5),
        'conv_w': nrm(ks[23], (L, CONV_W, 2 * D_FF), CONV_W ** -0.5),
        'conv_b': nrm(ks[24], (L, 2 * D_FF), 0.02),
        'w_down': nrm(ks[25], (L, D_FF, D_MODEL), D_FF ** -0.5),
        'norm_f_g': 1.0 + nrm(ks[26], (D_MODEL,), 0.02),
    }


def reference(x, norm1_g, w_in, mu_b, rwkv_w0, rwkv_w2, rwkv_a0, rwkv_a2, rwkv_g2, rwkv_kk, rwkv_ka,
              rwkv_rk, rwkv_ln_w, rwkv_ln_b, gmlp_ln_w, gmlp_ln_b, gmlp_ws, gmlp_bs, w_proj_a, w_proj_b,
              w_out, norm2_g, w_up, conv_w, conv_b, w_down, norm_f_g):
    for l in range(DEPTH):
        h = rmsnorm(x, norm1_g[l])
        proj = h @ w_in[l]
        a_u, a_v, feat_b, gate_a, gate_b = jnp.split(
            proj, [D_A, 2 * D_A, 2 * D_A + D_B_IN, 2 * D_A + D_B_IN + D_MODEL], axis=-1)
        y_a = gmlp_spatial_gating(jax.nn.gelu(a_u, approximate=False), jax.nn.gelu(a_v, approximate=False),
                                  gmlp_ln_w[l], gmlp_ln_b[l], gmlp_ws[l], gmlp_bs[l])
        y_b = rwkv7_time_mix(feat_b, mu_b[l], rwkv_w0[l], rwkv_w2[l], rwkv_a0[l], rwkv_a2[l], rwkv_g2[l],
                             rwkv_kk[l], rwkv_ka[l], rwkv_rk[l], rwkv_ln_w[l], rwkv_ln_b[l])
        merged = (jax.nn.sigmoid(gate_a) * (y_a @ w_proj_a[l])
                  + jax.nn.sigmoid(gate_b) * (y_b @ w_proj_b[l]))
        x = x + merged @ w_out[l]
        h = rmsnorm(x, norm2_g[l])
        up = causal_dwconv(h @ w_up[l], conv_w[l], conv_b[l])
        u_g, u_v = jnp.split(up, 2, axis=-1)
        x = x + (jax.nn.gelu(u_g, approximate=False) * u_v) @ w_down[l]
    return rmsnorm(x, norm_f_g)
```

```python
import functools

import jax
import jax.numpy as jnp
from jax import lax
from jax.experimental import pallas as pl
from jax.experimental.pallas import tpu as pltpu

F32 = jnp.float32
BF16 = jnp.bfloat16

RMS_EPS = 1e-6
LN_EPS = 1e-5
GN_EPS = 64e-5
KK_NORM_FLOOR = 1e-12

LANES = 128
MXU_WIDTH = 256
SUBLANES_F32 = 8
SUBLANES_BF16 = 16

SCAN_CHUNK = 64


def _cparams(semantics, vmem_mb):
    return pltpu.CompilerParams(dimension_semantics=semantics, vmem_limit_bytes=vmem_mb * 1024 * 1024)


def _rmsnorm(x, g):
    ms = jnp.mean(x * x, axis=-1, keepdims=True)
    return x * lax.rsqrt(ms + RMS_EPS) * g


def _gelu(x):
    return 0.5 * x * (1.0 + lax.erf(x * (0.5 ** 0.5)))


def _identity(x):
    return x


def _bdot(a, b):
    return jnp.dot(a.astype(BF16), b.astype(BF16), preferred_element_type=F32)


def _norm_matmul_kernel(x_ref, g_ref, w_ref, o_ref, h_ref, *, act):
    @pl.when(pl.program_id(1) == 0)
    def _():
        h_ref[...] = _rmsnorm(x_ref[...], g_ref[...]).astype(h_ref.dtype)

    acc = jnp.dot(h_ref[...], w_ref[...], preferred_element_type=F32)
    o_ref[...] = act(acc).astype(o_ref.dtype)


def _norm_matmul(x2, g, w, *, act, out_dtype, tm, tn):
    m, d = x2.shape
    n = w.shape[1]
    return pl.pallas_call(
        functools.partial(_norm_matmul_kernel, act=act),
        grid=(m // tm, n // tn),
        in_specs=[
            pl.BlockSpec((tm, d), lambda i, j: (i, 0)),
            pl.BlockSpec((1, d), lambda i, j: (0, 0)),
            pl.BlockSpec((d, tn), lambda i, j: (0, j)),
        ],
        out_specs=pl.BlockSpec((tm, tn), lambda i, j: (i, j)),
        out_shape=jax.ShapeDtypeStruct((m, n), out_dtype),
        scratch_shapes=[pltpu.VMEM((tm, d), BF16)],
        compiler_params=_cparams(("parallel", "arbitrary"), 48),
        name="norm_matmul",
    )(x2, g, w)


def _gmlp_kernel(u_ref, v_ref, lnw_ref, lnb_ref, ws_ref, bst_ref, wpa_ref, gate_ref, o_ref, ya_ref, *, chunk):
    tm, da = v_ref.shape
    n_groups = ws_ref.shape[0]
    gw = da // n_groups
    v = v_ref[...].astype(F32)
    mu = jnp.mean(v, axis=-1, keepdims=True)
    d = v - mu
    var = jnp.mean(d * d, axis=-1, keepdims=True)
    vn = (d * lax.rsqrt(var + LN_EPS) * lnw_ref[...] + lnb_ref[...]).astype(BF16)
    row = lax.broadcasted_iota(jnp.int32, (chunk, chunk), 0)
    col = lax.broadcasted_iota(jnp.int32, (chunk, chunk), 1)
    causal = row >= col
    for g in range(n_groups):
        wm = jnp.where(causal, ws_ref[g], 0.0).astype(BF16)
        bias = bst_ref[:, g:g + 1]
        for c in range(tm // chunk):
            rs = slice(c * chunk, (c + 1) * chunk)
            cs = slice(g * gw, (g + 1) * gw)
            mixed = jnp.dot(wm, vn[rs, cs], preferred_element_type=F32) + bias
            ya_ref[rs, cs] = (u_ref[rs, cs].astype(F32) * mixed).astype(BF16)
    pa = jnp.dot(ya_ref[...], wpa_ref[...], preferred_element_type=F32)
    o_ref[...] = (pa * gate_ref[...].astype(F32)).astype(o_ref.dtype)


def _gmlp(uv, gates, lnw, lnb, ws, bst, wpa, *, tm):
    m = uv.shape[0]
    da = uv.shape[1] // 2
    dm = wpa.shape[1]
    chunk = ws.shape[1]
    return pl.pallas_call(
        functools.partial(_gmlp_kernel, chunk=chunk),
        grid=(m // tm,),
        in_specs=[
            pl.BlockSpec((tm, da), lambda i: (i, 0)),
            pl.BlockSpec((tm, da), lambda i: (i, 1)),
            pl.BlockSpec((1, da), lambda i: (0, 0)),
            pl.BlockSpec((1, da), lambda i: (0, 0)),
            pl.BlockSpec(ws.shape, lambda i: (0, 0, 0)),
            pl.BlockSpec(bst.shape, lambda i: (0, 0)),
            pl.BlockSpec(wpa.shape, lambda i: (0, 0)),
            pl.BlockSpec((tm, dm), lambda i: (i, 0)),
        ],
        out_specs=pl.BlockSpec((tm, dm), lambda i: (i, 0)),
        out_shape=jax.ShapeDtypeStruct((m, dm), BF16),
        scratch_shapes=[pltpu.VMEM((tm, da), BF16)],
        compiler_params=_cparams(("parallel",), 48),
        name="gmlp",
    )(uv, uv, lnw, lnb, ws, bst, wpa, gates)


def _head_sum(x, bd):
    hi = x.astype(BF16)
    lo = (x - hi.astype(F32)).astype(BF16)
    outs = []
    for j in range(x.shape[1] // MXU_WIDTH):
        sl = slice(j * MXU_WIDTH, (j + 1) * MXU_WIDTH)
        outs.append(jnp.dot(hi[:, sl], bd, preferred_element_type=F32)
                    + jnp.dot(lo[:, sl], bd, preferred_element_type=F32))
    return jnp.concatenate(outs, axis=1)


def _prep_kernel(feat_ref, prev_ref, mu_ref, w0_ref, a0_ref, kk_ref, ka_ref, w01_ref, g2_ref, bd_ref,
                 r_o, lw_o, k_o, v_o, a_o, b_o, g_o, *, seq, db, lora_wa):
    tm = feat_ref.shape[0]
    i = pl.program_id(0)
    feat = feat_ref[...]
    first = (i * tm) % seq == 0
    prev = jnp.where(first, 0.0, prev_ref[SUBLANES_F32 - 1:SUBLANES_F32, :])
    rows = lax.broadcasted_iota(jnp.int32, (tm, 1), 0)
    shifted = jnp.where(rows == 0, prev, pltpu.roll(feat, 1, 0))
    fm = feat + (shifted - feat) * mu_ref[...]
    r = fm[:, 0:db]
    k = fm[:, db:2 * db]
    v = fm[:, 2 * db:3 * db]
    l0 = fm[:, 3 * db:3 * db + LANES]
    l1 = fm[:, 3 * db + LANES:]
    lane = lax.broadcasted_iota(jnp.int32, l0.shape, 1)
    x0 = jnp.where(lane < lora_wa, jnp.tanh(l0), l0)
    wa = _bdot(x0, w01_ref[...])
    g = _bdot(jax.nn.sigmoid(l1), g2_ref[...])
    w_raw = w0_ref[...] + wa[:, :db]
    a = jax.nn.sigmoid(a0_ref[...] + wa[:, db:])
    z = -w_raw
    softplus = jnp.maximum(z, 0.0) + jnp.log1p(jnp.exp(-jnp.abs(z)))
    w = -softplus - 0.5
    lw = -jnp.exp(w)
    kkv = k * kk_ref[...]
    nrm = jnp.sqrt(_head_sum(kkv * kkv, bd_ref[...]))
    kkn = kkv / jnp.maximum(nrm, KK_NORM_FLOOR)
    kp = k * (1.0 + (a - 1.0) * ka_ref[...])
    r_o[...] = r
    lw_o[...] = lw
    k_o[...] = kp
    v_o[...] = v
    a_o[...] = -kkn
    b_o[...] = kkn * a
    g_o[...] = g.astype(g_o.dtype)


def _rwkv_prep(feat, mu, w0, a0, kk, ka, w01, g2p, bd, *, seq, tm, lora_wa):
    m, nf = feat.shape
    db = w0.shape[1]
    vec = lambda n: pl.BlockSpec((1, n), lambda i: (0, 0))
    full = lambda a: pl.BlockSpec(a.shape, lambda i: (0, 0))
    out_spec = pl.BlockSpec((tm, db), lambda i: (i, 0))
    blocks_per_tile = tm // SUBLANES_F32
    return pl.pallas_call(
        functools.partial(_prep_kernel, seq=seq, db=db, lora_wa=lora_wa),
        grid=(m // tm,),
        in_specs=[
            pl.BlockSpec((tm, nf), lambda i: (i, 0)),
            pl.BlockSpec((SUBLANES_F32, nf), lambda i: (jnp.maximum(i * blocks_per_tile - 1, 0), 0)),
            vec(nf), vec(db), vec(db), vec(db), vec(db), full(w01), full(g2p), full(bd),
        ],
        out_specs=[out_spec] * 7,
        out_shape=[jax.ShapeDtypeStruct((m, db), F32)] * 6 + [jax.ShapeDtypeStruct((m, db), BF16)],
        compiler_params=_cparams(("parallel",), 48),
        name="rwkv_prep",
    )(feat, feat, mu, w0, a0, kk, ka, w01, g2p, bd)


def _mm(a, b, dims, mode):
    dn = (dims, ((), ()))
    if mode == "f32":
        return lax.dot_general(a, b, dn, precision=lax.Precision.HIGHEST, preferred_element_type=F32)
    return lax.dot_general(a.astype(BF16), b.astype(BF16), dn, preferred_element_type=F32)


_NN = ((1,), (0,))
_NT = ((1,), (1,))
_TN = ((0,), (0,))


def _scan_kernel(r_ref, lw_ref, k_ref, v_ref, a_ref, b_ref, y_ref, st_ref, *, head, mode):
    c = pl.program_id(1)
    C = lw_ref.shape[1]
    n = lw_ref.shape[2]
    pair = 2 * head
    assert pair == LANES and 2 * C == LANES

    @pl.when(c == 0)
    def _():
        st_ref[...] = jnp.zeros_like(st_ref)

    lw = lw_ref[0]
    rc = lax.broadcasted_iota(jnp.int32, (C, C), 0)
    cc = lax.broadcasted_iota(jnp.int32, (C, C), 1)
    tri = (rc >= cc).astype(F32)
    L = lax.dot_general(tri, lw, (_NN, ((), ())), precision=lax.Precision.HIGHEST, preferred_element_type=F32)
    LC = L[C - 1:C, :]
    p_in = jnp.exp(L)
    p_ex = jnp.exp(L - lw)
    p_inv = jnp.exp(-L)
    p_end = jnp.exp(LC - L)
    p_c = jnp.exp(LC)
    r = r_ref[0]
    k = k_ref[0]
    v = v_ref[0]
    a = a_ref[0]
    b = b_ref[0]
    rt = r * p_in
    at = a * p_ex
    bt = b * p_inv
    kt = k * p_inv
    bh = b * p_end
    kh = k * p_end

    lane = lax.broadcasted_iota(jnp.int32, (C, pair), 1)
    m0 = lane < head
    ri = lax.broadcasted_iota(jnp.int32, (pair, pair), 0)
    ci = lax.broadcasted_iota(jnp.int32, (pair, pair), 1)
    same = (ri >= C) == (ci >= C)
    strict = same & (ri > ci)
    incl = same & (ri >= ci)
    eye = ri == ci
    eye_f = eye.astype(F32)

    def stack(x):
        return jnp.concatenate([jnp.where(m0, x, 0.0), jnp.where(m0, 0.0, x)], axis=0)

    for p in range(n // pair):
        sl = slice(p * pair, (p + 1) * pair)
        ats, rts, bts, kts, vs, bhs, khs = (stack(x[:, sl]) for x in (at, rt, bt, kt, v, bh, kh))
        amat = _mm(jnp.concatenate([ats, rts], 0), jnp.concatenate([bts, kts], 0), _NT, mode)
        aab = jnp.where(strict, amat[:pair, :pair], 0.0)
        aak = jnp.where(strict, amat[:pair, pair:], 0.0)
        arb = jnp.where(incl, amat[pair:, :pair], 0.0)
        ark = jnp.where(incl, amat[pair:, pair:], 0.0)
        t_inv = eye_f + aab
        aj = _mm(aab, aab, _NN, mode)
        n_sq = C.bit_length() - 1
        for _ in range(n_sq - 2):
            res = _mm(aj, jnp.concatenate([t_inv, aj], 1), _NN, mode)
            t_inv = t_inv + res[:, :pair]
            aj = res[:, pair:]
        t_inv = t_inv + _mm(aj, t_inv, _NN, mode)
        wv = _mm(jnp.concatenate([aak, ark], 0), vs, _NN, mode)
        x = _mm(t_inv, jnp.concatenate([ats, wv[:pair]], 1), _NN, mode)
        z = _mm(arb, x, _NN, mode)
        rp_s = rts + z[:, :pair]
        yv_s = z[:, pair:] + wv[pair:]
        rp = rp_s[:C] + rp_s[C:]
        yv = yv_s[:C] + yv_s[C:]
        st0 = st_ref[p]
        y_ref[0, :, sl] = _mm(rp, st0, _NN, mode) + yv
        gmat = _mm(bhs, x, _TN, mode)
        hmat = _mm(khs, vs, _TN, mode)
        mt = jnp.where(eye, p_c[:, sl], 0.0) + gmat[:, :pair]
        nt = gmat[:, pair:] + hmat
        st_ref[p] = _mm(mt, st0, _NN, mode) + nt


def _wkv_scan(r, lw, k, v, a, b, *, head, mode):
    bsz, seq, n = r.shape
    C = SCAN_CHUNK
    spec = pl.BlockSpec((1, C, n), lambda i, c: (i, c, 0))
    return pl.pallas_call(
        functools.partial(_scan_kernel, head=head, mode=mode),
        grid=(bsz, seq // C),
        in_specs=[spec] * 6,
        out_specs=spec,
        out_shape=jax.ShapeDtypeStruct((bsz, seq, n), F32),
        scratch_shapes=[pltpu.VMEM((n // (2 * head), 2 * head, 2 * head), F32)],
        compiler_params=_cparams(("arbitrary", "arbitrary"), 48),
        name="wkv_scan",
    )(r, lw, k, v, a, b)


def _post_kernel(y_ref, r_ref, k_ref, v_ref, g_ref, rk_ref, lnw_ref, lnb_ref, bd_ref, wpb_ref, gate_ref,
                 pa_ref, wout_ref, x_ref, n2_ref, x1_o, h2_o, *, head):
    bd = bd_ref[...]
    y = y_ref[...]
    inv = 1.0 / head
    mean = _head_sum(y, bd) * inv
    d = y - mean
    var = _head_sum(d * d, bd) * inv
    yn = d * lax.rsqrt(var + GN_EPS) * lnw_ref[...] + lnb_ref[...]
    bonus = _head_sum(r_ref[...] * k_ref[...] * rk_ref[...], bd) * v_ref[...]
    yb = ((yn + bonus) * g_ref[...].astype(F32)).astype(BF16)
    pb = jnp.dot(yb, wpb_ref[...], preferred_element_type=F32) * gate_ref[...].astype(F32)
    merged = (pa_ref[...].astype(F32) + pb).astype(BF16)
    x1 = x_ref[...] + jnp.dot(merged, wout_ref[...], preferred_element_type=F32)
    x1_o[...] = x1
    h2_o[...] = _rmsnorm(x1, n2_ref[...]).astype(h2_o.dtype)


def _rwkv_post(y, r, k, v, g, rk, lnw, lnb, bd, wpb, gates, pa, wout, x2, n2, *, head, tm):
    m, db = y.shape
    dm = x2.shape[1]
    row = lambda n, j=0: pl.BlockSpec((tm, n), lambda i: (i, j))
    vec = lambda n: pl.BlockSpec((1, n), lambda i: (0, 0))
    full = lambda a: pl.BlockSpec(a.shape, lambda i: (0, 0))
    return pl.pallas_call(
        functools.partial(_post_kernel, head=head),
        grid=(m // tm,),
        in_specs=[row(db), row(db), row(db), row(db), row(db), vec(db), vec(db), vec(db), full(bd), full(wpb),
                  row(dm, 1), row(dm), full(wout), row(dm), vec(dm)],
        out_specs=[row(dm), row(dm)],
        out_shape=[jax.ShapeDtypeStruct((m, dm), F32), jax.ShapeDtypeStruct((m, dm), BF16)],
        compiler_params=_cparams(("parallel",), 56),
        name="rwkv_post",
    )(y, r, k, v, g, rk, lnw, lnb, bd, wpb, gates, pa, wout, x2, n2)


def _ffn_kernel(h_ref, hp_ref, x1_ref, wg_ref, wv_ref, cwg_ref, cwv_ref, cbg_ref, cbv_ref, wd_ref, nf_ref,
                o_ref, hext_ref, ug_ref, uv_ref, acc_ref, *, seq, taps):
    tm = h_ref.shape[0]
    halo = hp_ref.shape[0]
    i = pl.program_id(0)
    j = pl.program_id(1)

    @pl.when(j == 0)
    def _():
        first = (i * tm) % seq == 0
        hext_ref[0:halo, :] = jnp.where(first, jnp.zeros_like(hp_ref), hp_ref[...])
        hext_ref[halo:, :] = h_ref[...]
        acc_ref[...] = jnp.zeros_like(acc_ref)

    hext = hext_ref[...]
    ug_ref[...] = jnp.dot(hext, wg_ref[...], preferred_element_type=F32)
    uv_ref[...] = jnp.dot(hext, wv_ref[...], preferred_element_type=F32)

    def conv(u_ref, cw_ref, cb_ref):
        out = cb_ref[...]
        for t in range(taps):
            off = halo - (taps - 1) + t
            out = out + cw_ref[t:t + 1, :] * u_ref[off:off + tm, :]
        return out

    act = (_gelu(conv(ug_ref, cwg_ref, cbg_ref)) * conv(uv_ref, cwv_ref, cbv_ref)).astype(BF16)
    acc_ref[...] += jnp.dot(act, wd_ref[...], preferred_element_type=F32)

    @pl.when(j == pl.num_programs(1) - 1)
    def _():
        o_ref[...] = _rmsnorm(x1_ref[...] + acc_ref[...], nf_ref[...])


def _conv_ffn(h2, x1, wup, cw, cb, wd, nf, *, seq, tm, tn):
    m, dm = h2.shape
    dff = wd.shape[0]
    nj = dff // tn
    taps = cw.shape[0]
    halo = SUBLANES_BF16
    blocks_per_tile = tm // halo
    return pl.pallas_call(
        functools.partial(_ffn_kernel, seq=seq, taps=taps),
        grid=(m // tm, nj),
        in_specs=[
            pl.BlockSpec((tm, dm), lambda i, j: (i, 0)),
            pl.BlockSpec((halo, dm), lambda i, j: (jnp.maximum(i * blocks_per_tile - 1, 0), 0)),
            pl.BlockSpec((tm, dm), lambda i, j: (i, 0)),
            pl.BlockSpec((dm, tn), lambda i, j: (0, j)),
            pl.BlockSpec((dm, tn), lambda i, j: (0, nj + j)),
            pl.BlockSpec((taps, tn), lambda i, j: (0, j)),
            pl.BlockSpec((taps, tn), lambda i, j: (0, nj + j)),
            pl.BlockSpec((1, tn), lambda i, j: (0, j)),
            pl.BlockSpec((1, tn), lambda i, j: (0, nj + j)),
            pl.BlockSpec((tn, dm), lambda i, j: (j, 0)),
            pl.BlockSpec((1, dm), lambda i, j: (0, 0)),
        ],
        out_specs=pl.BlockSpec((tm, dm), lambda i, j: (i, 0)),
        out_shape=jax.ShapeDtypeStruct((m, dm), F32),
        scratch_shapes=[
            pltpu.VMEM((tm + halo, dm), BF16),
            pltpu.VMEM((tm + halo, tn), F32),
            pltpu.VMEM((tm + halo, tn), F32),
            pltpu.VMEM((tm, dm), F32),
        ],
        compiler_params=_cparams(("parallel", "arbitrary"), 56),
        name="conv_ffn",
    )(h2, h2, x1, wup, wup, cw, cw, cb, cb, wd, nf)


def _pick_tile(n, want):
    t = min(want, n)
    while n % t:
        t //= 2
    return t


def _pick_lane_tile(n, want):
    assert n % LANES == 0
    return max(t for t in range(LANES, want + 1, LANES) if n % t == 0)


def _block_diag_ones(width, block):
    idx = jnp.arange(width) // block
    return (idx[:, None] == idx[None, :]).astype(BF16)


def _layer(x2, seq, p, scan_mode):
    m, dm = x2.shape
    da = p["gmlp_ln_w"].shape[0]
    db = p["rwkv_w0"].shape[0]
    n_heads, head = p["rwkv_rk"].shape
    lora_w = p["rwkv_w2"].shape[0]
    lora_a = p["rwkv_a2"].shape[0]
    lora_g = p["rwkv_g2"].shape[0]
    assert lora_w + lora_a == LANES and lora_g <= MXU_WIDTH
    d_b_in = 3 * db + lora_w + lora_a + lora_g
    w_in = p["w_in"]
    nf_pad = 3 * db + LANES + MXU_WIDTH
    w_uv = w_in[:, :2 * da].astype(BF16)
    w_feat = jnp.pad(w_in[:, 2 * da:2 * da + d_b_in], ((0, 0), (0, nf_pad - d_b_in))).astype(BF16)
    w_gate = w_in[:, 2 * da + d_b_in:].astype(BF16)
    g1 = p["norm1_g"].reshape(1, dm)

    tm_big = _pick_tile(m, 1024)
    uv = _norm_matmul(x2, g1, w_uv, act=_gelu, out_dtype=BF16, tm=tm_big, tn=_pick_tile(2 * da, 512))
    feat = _norm_matmul(x2, g1, w_feat, act=_identity, out_dtype=F32, tm=tm_big, tn=_pick_lane_tile(nf_pad, 512))
    gates = _norm_matmul(x2, g1, w_gate, act=jax.nn.sigmoid, out_dtype=BF16, tm=tm_big, tn=_pick_tile(2 * dm, 512))

    pa = _gmlp(uv, gates, p["gmlp_ln_w"].reshape(1, da), p["gmlp_ln_b"].reshape(1, da), p["gmlp_ws"],
               p["gmlp_bs"].T, p["w_proj_a"].astype(BF16), tm=_pick_tile(seq, 512))

    mu = jnp.pad(p["mu_b"], (0, nf_pad - d_b_in)).reshape(1, nf_pad)
    zeros = jnp.zeros((lora_w, db), F32)
    w01 = jnp.concatenate([jnp.concatenate([p["rwkv_w2"], zeros], 1),
                           jnp.concatenate([jnp.zeros((lora_a, db), F32), p["rwkv_a2"]], 1)], 0).astype(BF16)
    g2p = jnp.pad(p["rwkv_g2"], ((0, MXU_WIDTH - lora_g), (0, 0))).astype(BF16)
    bd = _block_diag_ones(MXU_WIDTH, head)
    vecb = lambda a: a.reshape(1, db)
    r, lw, k, v, av, bv, g = _rwkv_prep(
        feat, mu, vecb(p["rwkv_w0"]), vecb(p["rwkv_a0"]), vecb(p["rwkv_kk"]), vecb(p["rwkv_ka"]), w01, g2p, bd,
        seq=seq, tm=_pick_tile(seq, 256), lora_wa=lora_w)
    bsz = m // seq
    to3 = lambda t: t.reshape(bsz, seq, db)
    y = _wkv_scan(to3(r), to3(lw), to3(k), to3(v), to3(av), to3(bv), head=head, mode=scan_mode).reshape(m, db)
    x1, h2 = _rwkv_post(y, r, k, v, g, vecb(p["rwkv_rk"]), vecb(p["rwkv_ln_w"]), vecb(p["rwkv_ln_b"]), bd,
                        p["w_proj_b"].astype(BF16), gates, pa, p["w_out"].astype(BF16), x2,
                        p["norm2_g"].reshape(1, dm), head=head, tm=_pick_tile(seq, 256))
    return x1, h2


def kernel(x, norm1_g, w_in, mu_b, rwkv_w0, rwkv_w2, rwkv_a0, rwkv_a2, rwkv_g2, rwkv_kk, rwkv_ka, rwkv_rk, rwkv_ln_w, rwkv_ln_b, gmlp_ln_w, gmlp_ln_b, gmlp_ws, gmlp_bs, w_proj_a, w_proj_b, w_out, norm2_g, w_up, conv_w, conv_b, w_down, norm_f_g):
    bsz, seq, dm = x.shape
    depth = w_in.shape[0]
    assert depth == 1, "the fused FFN applies the final rmsnorm; one layer is supported"
    x2 = x.reshape(bsz * seq, dm)
    l = 0
    p = dict(norm1_g=norm1_g[l], w_in=w_in[l], mu_b=mu_b[l], rwkv_w0=rwkv_w0[l], rwkv_w2=rwkv_w2[l],
             rwkv_a0=rwkv_a0[l], rwkv_a2=rwkv_a2[l], rwkv_g2=rwkv_g2[l], rwkv_kk=rwkv_kk[l], rwkv_ka=rwkv_ka[l],
             rwkv_rk=rwkv_rk[l], rwkv_ln_w=rwkv_ln_w[l], rwkv_ln_b=rwkv_ln_b[l], gmlp_ln_w=gmlp_ln_w[l],
             gmlp_ln_b=gmlp_ln_b[l], gmlp_ws=gmlp_ws[l], gmlp_bs=gmlp_bs[l], w_proj_a=w_proj_a[l],
             w_proj_b=w_proj_b[l], w_out=w_out[l], norm2_g=norm2_g[l])
    x1, h2 = _layer(x2, seq, p, "f32")
    dff = w_down.shape[1]
    out = _conv_ffn(h2, x1, w_up[l].astype(BF16), conv_w[l], conv_b[l].reshape(1, -1), w_down[l].astype(BF16),
                    norm_f_g.reshape(1, dm), seq=seq, tm=_pick_tile(seq, 512), tn=_pick_tile(dff // 1, 512))
    return out.reshape(bsz, seq, dm)
```

```python
import functools

import jax
import jax.numpy as jnp
from jax import lax
from jax.experimental import pallas as pl
from jax.experimental.pallas import tpu as pltpu

F32 = jnp.float32
BF16 = jnp.bfloat16

RMS_EPS = 1e-6
LN_EPS = 1e-5
GN_EPS = 64e-5
KK_NORM_FLOOR = 1e-12

LANES = 128
MXU_WIDTH = 256
SUBLANES_F32 = 8
SUBLANES_BF16 = 16

SCAN_CHUNK = 64


def _cparams(semantics, vmem_mb):
    return pltpu.CompilerParams(dimension_semantics=semantics, vmem_limit_bytes=vmem_mb * 1024 * 1024)


def _rmsnorm(x, g):
    ms = jnp.mean(x * x, axis=-1, keepdims=True)
    return x * lax.rsqrt(ms + RMS_EPS) * g


def _gelu(x):
    return 0.5 * x * (1.0 + lax.erf(x * (0.5 ** 0.5)))


def _identity(x):
    return x


def _bdot(a, b):
    return jnp.dot(a.astype(BF16), b.astype(BF16), preferred_element_type=F32)


def _norm_matmul_kernel(x_ref, g_ref, w_ref, o_ref, h_ref, *, act):
    @pl.when(pl.program_id(1) == 0)
    def _():
        h_ref[...] = _rmsnorm(x_ref[...], g_ref[...]).astype(h_ref.dtype)

    acc = jnp.dot(h_ref[...], w_ref[...], preferred_element_type=F32)
    o_ref[...] = act(acc).astype(o_ref.dtype)


def _norm_matmul(x2, g, w, *, act, out_dtype, tm, tn):
    m, d = x2.shape
    n = w.shape[1]
    return pl.pallas_call(
        functools.partial(_norm_matmul_kernel, act=act),
        grid=(m // tm, n // tn),
        in_specs=[
            pl.BlockSpec((tm, d), lambda i, j: (i, 0)),
            pl.BlockSpec((1, d), lambda i, j: (0, 0)),
            pl.BlockSpec((d, tn), lambda i, j: (0, j)),
        ],
        out_specs=pl.BlockSpec((tm, tn), lambda i, j: (i, j)),
        out_shape=jax.ShapeDtypeStruct((m, n), out_dtype),
        scratch_shapes=[pltpu.VMEM((tm, d), BF16)],
        compiler_params=_cparams(("parallel", "arbitrary"), 48),
        name="norm_matmul",
    )(x2, g, w)


def _gmlp_kernel(u_ref, v_ref, lnw_ref, lnb_ref, ws_ref, bst_ref, wpa_ref, gate_ref, o_ref, ya_ref, *, chunk):
    tm, da = v_ref.shape
    n_groups = ws_ref.shape[0]
    gw = da // n_groups
    v = v_ref[...].astype(F32)
    mu = jnp.mean(v, axis=-1, keepdims=True)
    d = v - mu
    var = jnp.mean(d * d, axis=-1, keepdims=True)
    vn = (d * lax.rsqrt(var + LN_EPS) * lnw_ref[...] + lnb_ref[...]).astype(BF16)
    row = lax.broadcasted_iota(jnp.int32, (chunk, chunk), 0)
    col = lax.broadcasted_iota(jnp.int32, (chunk, chunk), 1)
    causal = row >= col
    for g in range(n_groups):
        wm = jnp.where(causal, ws_ref[g], 0.0).astype(BF16)
        bias = bst_ref[:, g:g + 1]
        for c in range(tm // chunk):
            rs = slice(c * chunk, (c + 1) * chunk)
            cs = slice(g * gw, (g + 1) * gw)
            mixed = jnp.dot(wm, vn[rs, cs], preferred_element_type=F32) + bias
            ya_ref[rs, cs] = (u_ref[rs, cs].astype(F32) * mixed).astype(BF16)
    pa = jnp.dot(ya_ref[...], wpa_ref[...], preferred_element_type=F32)
    o_ref[...] = (pa * gate_ref[...].astype(F32)).astype(o_ref.dtype)


def _gmlp(uv, gates, lnw, lnb, ws, bst, wpa, *, tm):
    m = uv.shape[0]
    da = uv.shape[1] // 2
    dm = wpa.shape[1]
    chunk = ws.shape[1]
    return pl.pallas_call(
        functools.partial(_gmlp_kernel, chunk=chunk),
        grid=(m // tm,),
        in_specs=[
            pl.BlockSpec((tm, da), lambda i: (i, 0)),
            pl.BlockSpec((tm, da), lambda i: (i, 1)),
            pl.BlockSpec((1, da), lambda i: (0, 0)),
            pl.BlockSpec((1, da), lambda i: (0, 0)),
            pl.BlockSpec(ws.shape, lambda i: (0, 0, 0)),
            pl.BlockSpec(bst.shape, lambda i: (0, 0)),
            pl.BlockSpec(wpa.shape, lambda i: (0, 0)),
            pl.BlockSpec((tm, dm), lambda i: (i, 0)),
        ],
        out_specs=pl.BlockSpec((tm, dm), lambda i: (i, 0)),
        out_shape=jax.ShapeDtypeStruct((m, dm), BF16),
        scratch_shapes=[pltpu.VMEM((tm, da), BF16)],
        compiler_params=_cparams(("parallel",), 48),
        name="gmlp",
    )(uv, uv, lnw, lnb, ws, bst, wpa, gates)


def _head_sum(x, bd):
    hi = x.astype(BF16)
    lo = (x - hi.astype(F32)).astype(BF16)
    outs = []
    for j in range(x.shape[1] // MXU_WIDTH):
        sl = slice(j * MXU_WIDTH, (j + 1) * MXU_WIDTH)
        outs.append(jnp.dot(hi[:, sl], bd, preferred_element_type=F32)
                    + jnp.dot(lo[:, sl], bd, preferred_element_type=F32))
    return jnp.concatenate(outs, axis=1)


def _prep_kernel(feat_ref, prev_ref, mu_ref, w0_ref, a0_ref, kk_ref, ka_ref, w01_ref, g2_ref, bd_ref,
                 r_o, lw_o, k_o, v_o, a_o, b_o, g_o, *, seq, db, lora_wa):
    tm = feat_ref.shape[0]
    i = pl.program_id(0)
    feat = feat_ref[...]
    first = (i * tm) % seq == 0
    prev = jnp.where(first, 0.0, prev_ref[SUBLANES_F32 - 1:SUBLANES_F32, :])
    rows = lax.broadcasted_iota(jnp.int32, (tm, 1), 0)
    shifted = jnp.where(rows == 0, prev, pltpu.roll(feat, 1, 0))
    fm = feat + (shifted - feat) * mu_ref[...]
    r = fm[:, 0:db]
    k = fm[:, db:2 * db]
    v = fm[:, 2 * db:3 * db]
    l0 = fm[:, 3 * db:3 * db + LANES]
    l1 = fm[:, 3 * db + LANES:]
    lane = lax.broadcasted_iota(jnp.int32, l0.shape, 1)
    x0 = jnp.where(lane < lora_wa, jnp.tanh(l0), l0)
    wa = _bdot(x0, w01_ref[...])
    g = _bdot(jax.nn.sigmoid(l1), g2_ref[...])
    w_raw = w0_ref[...] + wa[:, :db]
    a = jax.nn.sigmoid(a0_ref[...] + wa[:, db:])
    z = -w_raw
    softplus = jnp.maximum(z, 0.0) + jnp.log1p(jnp.exp(-jnp.abs(z)))
    w = -softplus - 0.5
    lw = -jnp.exp(w)
    kkv = k * kk_ref[...]
    nrm = jnp.sqrt(_head_sum(kkv * kkv, bd_ref[...]))
    kkn = kkv / jnp.maximum(nrm, KK_NORM_FLOOR)
    kp = k * (1.0 + (a - 1.0) * ka_ref[...])
    r_o[...] = r
    lw_o[...] = lw
    k_o[...] = kp
    v_o[...] = v
    a_o[...] = -kkn
    b_o[...] = kkn * a
    g_o[...] = g.astype(g_o.dtype)


def _rwkv_prep(feat, mu, w0, a0, kk, ka, w01, g2p, bd, *, seq, tm, lora_wa):
    m, nf = feat.shape
    db = w0.shape[1]
    vec = lambda n: pl.BlockSpec((1, n), lambda i: (0, 0))
    full = lambda a: pl.BlockSpec(a.shape, lambda i: (0, 0))
    out_spec = pl.BlockSpec((tm, db), lambda i: (i, 0))
    blocks_per_tile = tm // SUBLANES_F32
    return pl.pallas_call(
        functools.partial(_prep_kernel, seq=seq, db=db, lora_wa=lora_wa),
        grid=(m // tm,),
        in_specs=[
            pl.BlockSpec((tm, nf), lambda i: (i, 0)),
            pl.BlockSpec((SUBLANES_F32, nf), lambda i: (jnp.maximum(i * blocks_per_tile - 1, 0), 0)),
            vec(nf), vec(db), vec(db), vec(db), vec(db), full(w01), full(g2p), full(bd),
        ],
        out_specs=[out_spec] * 7,
        out_shape=[jax.ShapeDtypeStruct((m, db), F32)] * 6 + [jax.ShapeDtypeStruct((m, db), BF16)],
        compiler_params=_cparams(("parallel",), 48),
        name="rwkv_prep",
    )(feat, feat, mu, w0, a0, kk, ka, w01, g2p, bd)


def _mm(a, b, dims):
    return lax.dot_general(a.astype(BF16), b.astype(BF16), (dims, ((), ())), preferred_element_type=F32)


_NN = ((1,), (0,))
_NT = ((1,), (1,))
_TN = ((0,), (0,))


def _scan_kernel(r_ref, lw_ref, k_ref, v_ref, a_ref, b_ref, y_ref, st_ref, *, head):
    c = pl.program_id(1)
    C = lw_ref.shape[1]
    n = lw_ref.shape[2]
    pair = 2 * head
    assert pair == LANES and 2 * C == LANES

    @pl.when(c == 0)
    def _():
        st_ref[...] = jnp.zeros_like(st_ref)

    lw = lw_ref[0]
    rc = lax.broadcasted_iota(jnp.int32, (C, C), 0)
    cc = lax.broadcasted_iota(jnp.int32, (C, C), 1)
    tri = (rc >= cc).astype(F32)
    L = lax.dot_general(tri, lw, (_NN, ((), ())), precision=lax.Precision.HIGHEST, preferred_element_type=F32)
    LC = L[C - 1:C, :]
    p_in = jnp.exp(L)
    p_ex = jnp.exp(L - lw)
    p_inv = jnp.exp(-L)
    p_end = jnp.exp(LC - L)
    p_c = jnp.exp(LC)
    r = r_ref[0]
    k = k_ref[0]
    v = v_ref[0]
    a = a_ref[0]
    b = b_ref[0]
    rt = r * p_in
    at = a * p_ex
    bt = b * p_inv
    kt = k * p_inv
    bh = b * p_end
    kh = k * p_end

    lane = lax.broadcasted_iota(jnp.int32, (C, pair), 1)
    m0 = lane < head
    ri = lax.broadcasted_iota(jnp.int32, (pair, pair), 0)
    ci = lax.broadcasted_iota(jnp.int32, (pair, pair), 1)
    same = (ri >= C) == (ci >= C)
    strict = same & (ri > ci)
    incl = same & (ri >= ci)
    eye = ri == ci
    eye_f = eye.astype(F32)

    def stack(x):
        return jnp.concatenate([jnp.where(m0, x, 0.0), jnp.where(m0, 0.0, x)], axis=0)

    pairs = range(n // pair)
    sls = [slice(p * pair, (p + 1) * pair) for p in pairs]
    ats, rts, bts, kts, vs, bhs, khs = ([stack(x[:, sl]) for sl in sls] for x in (at, rt, bt, kt, v, bh, kh))
    amat = [_mm(jnp.concatenate([ats[p], rts[p]], 0), jnp.concatenate([bts[p], kts[p]], 0), _NT) for p in pairs]
    aab = [jnp.where(strict, amat[p][:pair, :pair], 0.0) for p in pairs]
    aak = [jnp.where(strict, amat[p][:pair, pair:], 0.0) for p in pairs]
    arb = [jnp.where(incl, amat[p][pair:, :pair], 0.0) for p in pairs]
    ark = [jnp.where(incl, amat[p][pair:, pair:], 0.0) for p in pairs]
    t_inv = [eye_f + aab[p] for p in pairs]
    aj = [_mm(aab[p], aab[p], _NN) for p in pairs]
    wv = [_mm(jnp.concatenate([aak[p], ark[p]], 0), vs[p], _NN) for p in pairs]
    hmat = [_mm(khs[p], vs[p], _TN) for p in pairs]
    n_factors = C.bit_length() - 1
    for _ in range(n_factors - 2):
        res = [_mm(aj[p], jnp.concatenate([t_inv[p], aj[p]], 1), _NN) for p in pairs]
        t_inv = [t_inv[p] + res[p][:, :pair] for p in pairs]
        aj = [res[p][:, pair:] for p in pairs]
    t_inv = [t_inv[p] + _mm(aj[p], t_inv[p], _NN) for p in pairs]
    x = [_mm(t_inv[p], jnp.concatenate([ats[p], wv[p][:pair]], 1), _NN) for p in pairs]
    z = [_mm(arb[p], x[p], _NN) for p in pairs]
    gmat = [_mm(bhs[p], x[p], _TN) for p in pairs]
    st0 = [st_ref[p] for p in pairs]
    for p in pairs:
        mt = jnp.where(eye, p_c[:, sls[p]], 0.0) + gmat[p][:, :pair]
        nt = gmat[p][:, pair:] + hmat[p]
        st_ref[p] = _mm(mt, st0[p], _NN) + nt
    for p in pairs:
        rp_s = rts[p] + z[p][:, :pair]
        yv_s = z[p][:, pair:] + wv[p][pair:]
        rp = rp_s[:C] + rp_s[C:]
        yv = yv_s[:C] + yv_s[C:]
        y_ref[0, :, sls[p]] = _mm(rp, st0[p], _NN) + yv


def _wkv_scan(r, lw, k, v, a, b, *, head):
    bsz, seq, n = r.shape
    C = SCAN_CHUNK
    spec = pl.BlockSpec((1, C, n), lambda i, c: (i, c, 0))
    return pl.pallas_call(
        functools.partial(_scan_kernel, head=head),
        grid=(bsz, seq // C),
        in_specs=[spec] * 6,
        out_specs=spec,
        out_shape=jax.ShapeDtypeStruct((bsz, seq, n), F32),
        scratch_shapes=[pltpu.VMEM((n // (2 * head), 2 * head, 2 * head), F32)],
        compiler_params=_cparams(("arbitrary", "arbitrary"), 48),
        name="wkv_scan",
    )(r, lw, k, v, a, b)


def _post_kernel(y_ref, r_ref, k_ref, v_ref, g_ref, rk_ref, lnw_ref, lnb_ref, bd_ref, wpb_ref, gate_ref,
                 pa_ref, wout_ref, x_ref, n2_ref, x1_o, h2_o, *, head):
    bd = bd_ref[...]
    y = y_ref[...]
    inv = 1.0 / head
    mean = _head_sum(y, bd) * inv
    d = y - mean
    var = _head_sum(d * d, bd) * inv
    yn = d * lax.rsqrt(var + GN_EPS) * lnw_ref[...] + lnb_ref[...]
    bonus = _head_sum(r_ref[...] * k_ref[...] * rk_ref[...], bd) * v_ref[...]
    yb = ((yn + bonus) * g_ref[...].astype(F32)).astype(BF16)
    pb = jnp.dot(yb, wpb_ref[...], preferred_element_type=F32) * gate_ref[...].astype(F32)
    merged = (pa_ref[...].astype(F32) + pb).astype(BF16)
    x1 = x_ref[...] + jnp.dot(merged, wout_ref[...], preferred_element_type=F32)
    x1_o[...] = x1
    h2_o[...] = _rmsnorm(x1, n2_ref[...]).astype(h2_o.dtype)


def _rwkv_post(y, r, k, v, g, rk, lnw, lnb, bd, wpb, gates, pa, wout, x2, n2, *, head, tm):
    m, db = y.shape
    dm = x2.shape[1]
    row = lambda n, j=0: pl.BlockSpec((tm, n), lambda i: (i, j))
    vec = lambda n: pl.BlockSpec((1, n), lambda i: (0, 0))
    full = lambda a: pl.BlockSpec(a.shape, lambda i: (0, 0))
    return pl.pallas_call(
        functools.partial(_post_kernel, head=head),
        grid=(m // tm,),
        in_specs=[row(db), row(db), row(db), row(db), row(db), vec(db), vec(db), vec(db), full(bd), full(wpb),
                  row(dm, 1), row(dm), full(wout), row(dm), vec(dm)],
        out_specs=[row(dm), row(dm)],
        out_shape=[jax.ShapeDtypeStruct((m, dm), F32), jax.ShapeDtypeStruct((m, dm), BF16)],
        compiler_params=_cparams(("parallel",), 56),
        name="rwkv_post",
    )(y, r, k, v, g, rk, lnw, lnb, bd, wpb, gates, pa, wout, x2, n2)


def _ffn_kernel(h_ref, hp_ref, x1_ref, wg_ref, wv_ref, cwg_ref, cwv_ref, cbg_ref, cbv_ref, wd_ref, nf_ref,
                o_ref, hext_ref, ug_ref, uv_ref, acc_ref, *, seq, taps):
    tm = h_ref.shape[0]
    halo = hp_ref.shape[0]
    i = pl.program_id(0)
    j = pl.program_id(1)

    @pl.when(j == 0)
    def _():
        first = (i * tm) % seq == 0
        hext_ref[0:halo, :] = jnp.where(first, jnp.zeros_like(hp_ref), hp_ref[...])
        hext_ref[halo:, :] = h_ref[...]
        acc_ref[...] = jnp.zeros_like(acc_ref)

    hext = hext_ref[...]
    ug_ref[...] = jnp.dot(hext, wg_ref[...], preferred_element_type=F32)
    uv_ref[...] = jnp.dot(hext, wv_ref[...], preferred_element_type=F32)

    def conv(u_ref, cw_ref, cb_ref):
        out = cb_ref[...]
        for t in range(taps):
            off = halo - (taps - 1) + t
            out = out + cw_ref[t:t + 1, :] * u_ref[off:off + tm, :]
        return out

    act = (_gelu(conv(ug_ref, cwg_ref, cbg_ref)) * conv(uv_ref, cwv_ref, cbv_ref)).astype(BF16)
    acc_ref[...] += jnp.dot(act, wd_ref[...], preferred_element_type=F32)

    @pl.when(j == pl.num_programs(1) - 1)
    def _():
        o_ref[...] = _rmsnorm(x1_ref[...] + acc_ref[...], nf_ref[...])


def _conv_ffn(h2, x1, wup, cw, cb, wd, nf, *, seq, tm, tn):
    m, dm = h2.shape
    dff = wd.shape[0]
    nj = dff // tn
    taps = cw.shape[0]
    halo = SUBLANES_BF16
    blocks_per_tile = tm // halo
    return pl.pallas_call(
        functools.partial(_ffn_kernel, seq=seq, taps=taps),
        grid=(m // tm, nj),
        in_specs=[
            pl.BlockSpec((tm, dm), lambda i, j: (i, 0)),
            pl.BlockSpec((halo, dm), lambda i, j: (jnp.maximum(i * blocks_per_tile - 1, 0), 0)),
            pl.BlockSpec((tm, dm), lambda i, j: (i, 0)),
            pl.BlockSpec((dm, tn), lambda i, j: (0, j)),
            pl.BlockSpec((dm, tn), lambda i, j: (0, nj + j)),
            pl.BlockSpec((taps, tn), lambda i, j: (0, j)),
            pl.BlockSpec((taps, tn), lambda i, j: (0, nj + j)),
            pl.BlockSpec((1, tn), lambda i, j: (0, j)),
            pl.BlockSpec((1, tn), lambda i, j: (0, nj + j)),
            pl.BlockSpec((tn, dm), lambda i, j: (j, 0)),
            pl.BlockSpec((1, dm), lambda i, j: (0, 0)),
        ],
        out_specs=pl.BlockSpec((tm, dm), lambda i, j: (i, 0)),
        out_shape=jax.ShapeDtypeStruct((m, dm), F32),
        scratch_shapes=[
            pltpu.VMEM((tm + halo, dm), BF16),
            pltpu.VMEM((tm + halo, tn), F32),
            pltpu.VMEM((tm + halo, tn), F32),
            pltpu.VMEM((tm, dm), F32),
        ],
        compiler_params=_cparams(("parallel", "arbitrary"), 56),
        name="conv_ffn",
    )(h2, h2, x1, wup, wup, cw, cw, cb, cb, wd, nf)


def _pick_tile(n, want):
    t = min(want, n)
    while n % t:
        t //= 2
    return t


def _pick_lane_tile(n, want):
    assert n % LANES == 0
    return max(t for t in range(LANES, want + 1, LANES) if n % t == 0)


def _block_diag_ones(width, block):
    idx = jnp.arange(width) // block
    return (idx[:, None] == idx[None, :]).astype(BF16)


def _layer(x2, seq, p):
    m, dm = x2.shape
    da = p["gmlp_ln_w"].shape[0]
    db = p["rwkv_w0"].shape[0]
    n_heads, head = p["rwkv_rk"].shape
    lora_w = p["rwkv_w2"].shape[0]
    lora_a = p["rwkv_a2"].shape[0]
    lora_g = p["rwkv_g2"].shape[0]
    assert lora_w + lora_a == LANES and lora_g <= MXU_WIDTH
    d_b_in = 3 * db + lora_w + lora_a + lora_g
    w_in = p["w_in"]
    nf_pad = 3 * db + LANES + MXU_WIDTH
    w_uv = w_in[:, :2 * da].astype(BF16)
    w_feat = jnp.pad(w_in[:, 2 * da:2 * da + d_b_in], ((0, 0), (0, nf_pad - d_b_in))).astype(BF16)
    w_gate = w_in[:, 2 * da + d_b_in:].astype(BF16)
    g1 = p["norm1_g"].reshape(1, dm)

    tm_big = _pick_tile(m, 1024)
    uv = _norm_matmul(x2, g1, w_uv, act=_gelu, out_dtype=BF16, tm=tm_big, tn=_pick_tile(2 * da, 512))
    feat = _norm_matmul(x2, g1, w_feat, act=_identity, out_dtype=F32, tm=tm_big, tn=_pick_lane_tile(nf_pad, 512))
    gates = _norm_matmul(x2, g1, w_gate, act=jax.nn.sigmoid, out_dtype=BF16, tm=tm_big, tn=_pick_tile(2 * dm, 512))

    pa = _gmlp(uv, gates, p["gmlp_ln_w"].reshape(1, da), p["gmlp_ln_b"].reshape(1, da), p["gmlp_ws"],
               p["gmlp_bs"].T, p["w_proj_a"].astype(BF16), tm=_pick_tile(seq, 512))

    mu = jnp.pad(p["mu_b"], (0, nf_pad - d_b_in)).reshape(1, nf_pad)
    zeros = jnp.zeros((lora_w, db), F32)
    w01 = jnp.concatenate([jnp.concatenate([p["rwkv_w2"], zeros], 1),
                           jnp.concatenate([jnp.zeros((lora_a, db), F32), p["rwkv_a2"]], 1)], 0).astype(BF16)
    g2p = jnp.pad(p["rwkv_g2"], ((0, MXU_WIDTH - lora_g), (0, 0))).astype(BF16)
    bd = _block_diag_ones(MXU_WIDTH, head)
    vecb = lambda a: a.reshape(1, db)
    r, lw, k, v, av, bv, g = _rwkv_prep(
        feat, mu, vecb(p["rwkv_w0"]), vecb(p["rwkv_a0"]), vecb(p["rwkv_kk"]), vecb(p["rwkv_ka"]), w01, g2p, bd,
        seq=seq, tm=_pick_tile(seq, 256), lora_wa=lora_w)
    bsz = m // seq
    to3 = lambda t: t.reshape(bsz, seq, db)
    y = _wkv_scan(to3(r), to3(lw), to3(k), to3(v), to3(av), to3(bv), head=head).reshape(m, db)
    x1, h2 = _rwkv_post(y, r, k, v, g, vecb(p["rwkv_rk"]), vecb(p["rwkv_ln_w"]), vecb(p["rwkv_ln_b"]), bd,
                        p["w_proj_b"].astype(BF16), gates, pa, p["w_out"].astype(BF16), x2,
                        p["norm2_g"].reshape(1, dm), head=head, tm=_pick_tile(seq, 256))
    return x1, h2


def kernel(x, norm1_g, w_in, mu_b, rwkv_w0, rwkv_w2, rwkv_a0, rwkv_a2, rwkv_g2, rwkv_kk, rwkv_ka, rwkv_rk, rwkv_ln_w, rwkv_ln_b, gmlp_ln_w, gmlp_ln_b, gmlp_ws, gmlp_bs, w_proj_a, w_proj_b, w_out, norm2_g, w_up, conv_w, conv_b, w_down, norm_f_g):
    bsz, seq, dm = x.shape
    depth = w_in.shape[0]
    assert depth == 1, "the fused FFN applies the final rmsnorm; one layer is supported"
    x2 = x.reshape(bsz * seq, dm)
    l = 0
    p = dict(norm1_g=norm1_g[l], w_in=w_in[l], mu_b=mu_b[l], rwkv_w0=rwkv_w0[l], rwkv_w2=rwkv_w2[l],
             rwkv_a0=rwkv_a0[l], rwkv_a2=rwkv_a2[l], rwkv_g2=rwkv_g2[l], rwkv_kk=rwkv_kk[l], rwkv_ka=rwkv_ka[l],
             rwkv_rk=rwkv_rk[l], rwkv_ln_w=rwkv_ln_w[l], rwkv_ln_b=rwkv_ln_b[l], gmlp_ln_w=gmlp_ln_w[l],
             gmlp_ln_b=gmlp_ln_b[l], gmlp_ws=gmlp_ws[l], gmlp_bs=gmlp_bs[l], w_proj_a=w_proj_a[l],
             w_proj_b=w_proj_b[l], w_out=w_out[l], norm2_g=norm2_g[l])
    x1, h2 = _layer(x2, seq, p)
    dff = w_down.shape[1]
    out = _conv_ffn(h2, x1, w_up[l].astype(BF16), conv_w[l], conv_b[l].reshape(1, -1), w_down[l].astype(BF16),
                    norm_f_g.reshape(1, dm), seq=seq, tm=_pick_tile(seq, 512), tn=_pick_tile(dff // 1, 512))
    return out.reshape(bsz, seq, dm)
```

```python
import functools

import jax
import jax.numpy as jnp
from jax import lax
from jax.experimental import pallas as pl
from jax.experimental.pallas import tpu as pltpu

F32 = jnp.float32
BF16 = jnp.bfloat16

RMS_EPS = 1e-6
LN_EPS = 1e-5
GN_EPS = 64e-5
KK_NORM_FLOOR = 1e-12

LANES = 128
MXU_WIDTH = 256
SUBLANES_F32 = 8
SUBLANES_BF16 = 16

SCAN_CHUNK = 64


def _cparams(semantics, vmem_mb):
    return pltpu.CompilerParams(dimension_semantics=semantics, vmem_limit_bytes=vmem_mb * 1024 * 1024)


def _rmsnorm(x, g):
    ms = jnp.mean(x * x, axis=-1, keepdims=True)
    return x * lax.rsqrt(ms + RMS_EPS) * g


def _gelu(x):
    return 0.5 * x * (1.0 + lax.erf(x * (0.5 ** 0.5)))


def _bdot(a, b):
    return jnp.dot(a.astype(BF16), b.astype(BF16), preferred_element_type=F32)


def _inproj_kernel(x_ref, g_ref, w_ref, uv_ref, feat_ref, gate_ref, h_ref, *, n_uv, n_feat):
    j = pl.program_id(1)

    @pl.when(j == 0)
    def _():
        h_ref[...] = _rmsnorm(x_ref[...], g_ref[...]).astype(h_ref.dtype)

    def project():
        return jnp.dot(h_ref[...], w_ref[...], preferred_element_type=F32)

    @pl.when(j < n_uv)
    def _():
        uv_ref[...] = _gelu(project()).astype(uv_ref.dtype)

    @pl.when((j >= n_uv) & (j < n_uv + n_feat))
    def _():
        feat_ref[...] = project()

    @pl.when(j >= n_uv + n_feat)
    def _():
        gate_ref[...] = jax.nn.sigmoid(project()).astype(gate_ref.dtype)


def _inproj(x2, g, w, *, widths, tm, tn):
    m, d = x2.shape
    n_uv, n_feat, n_gate = (wd // tn for wd in widths)
    assert all(wd % tn == 0 for wd in widths) and sum(widths) == w.shape[1]
    return pl.pallas_call(
        functools.partial(_inproj_kernel, n_uv=n_uv, n_feat=n_feat),
        grid=(m // tm, n_uv + n_feat + n_gate),
        in_specs=[
            pl.BlockSpec((tm, d), lambda i, j: (i, 0)),
            pl.BlockSpec((1, d), lambda i, j: (0, 0)),
            pl.BlockSpec((d, tn), lambda i, j: (0, j)),
        ],
        out_specs=[
            pl.BlockSpec((tm, tn), lambda i, j: (i, jnp.minimum(j, n_uv - 1))),
            pl.BlockSpec((tm, tn), lambda i, j: (i, jnp.clip(j - n_uv, 0, n_feat - 1))),
            pl.BlockSpec((tm, tn), lambda i, j: (i, jnp.clip(j - n_uv - n_feat, 0, n_gate - 1))),
        ],
        out_shape=[
            jax.ShapeDtypeStruct((m, widths[0]), BF16),
            jax.ShapeDtypeStruct((m, widths[1]), F32),
            jax.ShapeDtypeStruct((m, widths[2]), BF16),
        ],
        scratch_shapes=[pltpu.VMEM((tm, d), BF16)],
        compiler_params=_cparams(("parallel", "arbitrary"), 48),
        name="inproj",
    )(x2, g, w)


def _gmlp_kernel(u_ref, v_ref, lnw_ref, lnb_ref, ws_ref, bst_ref, wpa_ref, gate_ref, o_ref, ya_ref, *, chunk):
    tm, da = v_ref.shape
    n_groups = ws_ref.shape[0]
    gw = da // n_groups
    v = v_ref[...].astype(F32)
    mu = jnp.mean(v, axis=-1, keepdims=True)
    d = v - mu
    var = jnp.mean(d * d, axis=-1, keepdims=True)
    vn = (d * lax.rsqrt(var + LN_EPS) * lnw_ref[...] + lnb_ref[...]).astype(BF16)
    row = lax.broadcasted_iota(jnp.int32, (chunk, chunk), 0)
    col = lax.broadcasted_iota(jnp.int32, (chunk, chunk), 1)
    causal = row >= col
    for g in range(n_groups):
        wm = jnp.where(causal, ws_ref[g], 0.0).astype(BF16)
        bias = bst_ref[:, g:g + 1]
        for c in range(tm // chunk):
            rs = slice(c * chunk, (c + 1) * chunk)
            cs = slice(g * gw, (g + 1) * gw)
            mixed = jnp.dot(wm, vn[rs, cs], preferred_element_type=F32) + bias
            ya_ref[rs, cs] = (u_ref[rs, cs].astype(F32) * mixed).astype(BF16)
    pa = jnp.dot(ya_ref[...], wpa_ref[...], preferred_element_type=F32)
    o_ref[...] = (pa * gate_ref[...].astype(F32)).astype(o_ref.dtype)


def _gmlp(uv, gates, lnw, lnb, ws, bst, wpa, *, tm):
    m = uv.shape[0]
    da = uv.shape[1] // 2
    dm = wpa.shape[1]
    chunk = ws.shape[1]
    return pl.pallas_call(
        functools.partial(_gmlp_kernel, chunk=chunk),
        grid=(m // tm,),
        in_specs=[
            pl.BlockSpec((tm, da), lambda i: (i, 0)),
            pl.BlockSpec((tm, da), lambda i: (i, 1)),
            pl.BlockSpec((1, da), lambda i: (0, 0)),
            pl.BlockSpec((1, da), lambda i: (0, 0)),
            pl.BlockSpec(ws.shape, lambda i: (0, 0, 0)),
            pl.BlockSpec(bst.shape, lambda i: (0, 0)),
            pl.BlockSpec(wpa.shape, lambda i: (0, 0)),
            pl.BlockSpec((tm, dm), lambda i: (i, 0)),
        ],
        out_specs=pl.BlockSpec((tm, dm), lambda i: (i, 0)),
        out_shape=jax.ShapeDtypeStruct((m, dm), BF16),
        scratch_shapes=[pltpu.VMEM((tm, da), BF16)],
        compiler_params=_cparams(("parallel",), 48),
        name="gmlp",
    )(uv, uv, lnw, lnb, ws, bst, wpa, gates)


def _head_sum(x, bd):
    hi = x.astype(BF16)
    lo = (x - hi.astype(F32)).astype(BF16)
    outs = []
    for j in range(x.shape[1] // MXU_WIDTH):
        sl = slice(j * MXU_WIDTH, (j + 1) * MXU_WIDTH)
        outs.append(jnp.dot(hi[:, sl], bd, preferred_element_type=F32)
                    + jnp.dot(lo[:, sl], bd, preferred_element_type=F32))
    return jnp.concatenate(outs, axis=1)


def _prep_kernel(feat_ref, prev_ref, mu_ref, w0_ref, a0_ref, kk_ref, ka_ref, w01_ref, g2_ref, bd_ref,
                 r_o, lw_o, k_o, v_o, a_o, b_o, g_o, *, seq, db, lora_wa):
    tm = feat_ref.shape[0]
    i = pl.program_id(0)
    feat = feat_ref[...]
    first = (i * tm) % seq == 0
    prev = jnp.where(first, 0.0, prev_ref[SUBLANES_F32 - 1:SUBLANES_F32, :])
    rows = lax.broadcasted_iota(jnp.int32, (tm, 1), 0)
    shifted = jnp.where(rows == 0, prev, pltpu.roll(feat, 1, 0))
    fm = feat + (shifted - feat) * mu_ref[...]
    r = fm[:, 0:db]
    k = fm[:, db:2 * db]
    v = fm[:, 2 * db:3 * db]
    l0 = fm[:, 3 * db:3 * db + LANES]
    l1 = fm[:, 3 * db + LANES:]
    lane = lax.broadcasted_iota(jnp.int32, l0.shape, 1)
    x0 = jnp.where(lane < lora_wa, jnp.tanh(l0), l0)
    wa = _bdot(x0, w01_ref[...])
    g = _bdot(jax.nn.sigmoid(l1), g2_ref[...])
    w_raw = w0_ref[...] + wa[:, :db]
    a = jax.nn.sigmoid(a0_ref[...] + wa[:, db:])
    z = -w_raw
    softplus = jnp.maximum(z, 0.0) + jnp.log1p(jnp.exp(-jnp.abs(z)))
    w = -softplus - 0.5
    lw = -jnp.exp(w)
    kkv = k * kk_ref[...]
    nrm = jnp.sqrt(_head_sum(kkv * kkv, bd_ref[...]))
    kkn = kkv / jnp.maximum(nrm, KK_NORM_FLOOR)
    kp = k * (1.0 + (a - 1.0) * ka_ref[...])
    lw_o[...] = lw
    for o_ref, val in ((r_o, r), (k_o, kp), (v_o, v), (a_o, -kkn), (b_o, kkn * a), (g_o, g)):
        o_ref[...] = val.astype(o_ref.dtype)


def _rwkv_prep(feat, mu, w0, a0, kk, ka, w01, g2p, bd, *, seq, tm, lora_wa):
    m = feat.shape[0]
    nf = mu.shape[1]
    db = w0.shape[1]
    vec = lambda n: pl.BlockSpec((1, n), lambda i: (0, 0))
    full = lambda a: pl.BlockSpec(a.shape, lambda i: (0, 0))
    out_spec = pl.BlockSpec((tm, db), lambda i: (i, 0))
    blocks_per_tile = tm // SUBLANES_F32
    return pl.pallas_call(
        functools.partial(_prep_kernel, seq=seq, db=db, lora_wa=lora_wa),
        grid=(m // tm,),
        in_specs=[
            pl.BlockSpec((tm, nf), lambda i: (i, 0)),
            pl.BlockSpec((SUBLANES_F32, nf), lambda i: (jnp.maximum(i * blocks_per_tile - 1, 0), 0)),
            vec(nf), vec(db), vec(db), vec(db), vec(db), full(w01), full(g2p), full(bd),
        ],
        out_specs=[out_spec] * 7,
        out_shape=[jax.ShapeDtypeStruct((m, db), F32 if i == 1 else BF16) for i in range(7)],
        compiler_params=_cparams(("parallel",), 48),
        name="rwkv_prep",
    )(feat, feat, mu, w0, a0, kk, ka, w01, g2p, bd)


def _mm(a, b, dims):
    return lax.dot_general(a.astype(BF16), b.astype(BF16), (dims, ((), ())), preferred_element_type=F32)


_NN = ((1,), (0,))
_NT = ((1,), (1,))
_TN = ((0,), (0,))


def _scan_kernel(r_ref, lw_ref, k_ref, v_ref, a_ref, b_ref, y_ref, st_ref, *, head):
    c = pl.program_id(1)
    C = lw_ref.shape[1]
    n = lw_ref.shape[2]
    pair = 2 * head
    assert pair == LANES and 2 * C == LANES

    @pl.when(c == 0)
    def _():
        st_ref[...] = jnp.zeros_like(st_ref)

    lw = lw_ref[0]
    rc = lax.broadcasted_iota(jnp.int32, (C, C), 0)
    cc = lax.broadcasted_iota(jnp.int32, (C, C), 1)
    tri = (rc >= cc).astype(F32)
    L = lax.dot_general(tri, lw, (_NN, ((), ())), precision=lax.Precision.HIGHEST, preferred_element_type=F32)
    LC = L[C - 1:C, :]
    p_in = jnp.exp(L)
    p_ex = jnp.exp(L - lw)
    p_inv = jnp.exp(-L)
    p_end = jnp.exp(LC - L)
    p_c = jnp.exp(LC)
    r, k, v, a, b = (ref[0].astype(F32) for ref in (r_ref, k_ref, v_ref, a_ref, b_ref))
    rt = r * p_in
    at = a * p_ex
    bt = b * p_inv
    kt = k * p_inv
    bh = b * p_end
    kh = k * p_end

    lane = lax.broadcasted_iota(jnp.int32, (C, pair), 1)
    m0 = lane < head
    ri = lax.broadcasted_iota(jnp.int32, (pair, pair), 0)
    ci = lax.broadcasted_iota(jnp.int32, (pair, pair), 1)
    same = (ri >= C) == (ci >= C)
    strict = same & (ri > ci)
    incl = same & (ri >= ci)
    eye = ri == ci
    eye_f = eye.astype(F32)

    def stack(x):
        return jnp.concatenate([jnp.where(m0, x, 0.0), jnp.where(m0, 0.0, x)], axis=0)

    pairs = range(n // pair)
    sls = [slice(p * pair, (p + 1) * pair) for p in pairs]
    ats, rts, bts, kts, vs, bhs, khs = ([stack(x[:, sl]) for sl in sls] for x in (at, rt, bt, kt, v, bh, kh))
    amat = [_mm(jnp.concatenate([ats[p], rts[p]], 0), jnp.concatenate([bts[p], kts[p]], 0), _NT) for p in pairs]
    aab = [jnp.where(strict, amat[p][:pair, :pair], 0.0) for p in pairs]
    aak = [jnp.where(strict, amat[p][:pair, pair:], 0.0) for p in pairs]
    arb = [jnp.where(incl, amat[p][pair:, :pair], 0.0) for p in pairs]
    ark = [jnp.where(incl, amat[p][pair:, pair:], 0.0) for p in pairs]
    t_inv = [eye_f + aab[p] for p in pairs]
    aj = [_mm(aab[p], aab[p], _NN) for p in pairs]
    wv = [_mm(jnp.concatenate([aak[p], ark[p]], 0), vs[p], _NN) for p in pairs]
    hmat = [_mm(khs[p], vs[p], _TN) for p in pairs]
    n_factors = C.bit_length() - 1
    for _ in range(n_factors - 2):
        res = [_mm(aj[p], jnp.concatenate([t_inv[p], aj[p]], 1), _NN) for p in pairs]
        t_inv = [t_inv[p] + res[p][:, :pair] for p in pairs]
        aj = [res[p][:, pair:] for p in pairs]
    t_inv = [t_inv[p] + _mm(aj[p], t_inv[p], _NN) for p in pairs]
    x = [_mm(t_inv[p], jnp.concatenate([ats[p], wv[p][:pair]], 1), _NN) for p in pairs]
    z = [_mm(arb[p], x[p], _NN) for p in pairs]
    gmat = [_mm(bhs[p], x[p], _TN) for p in pairs]
    st0 = [st_ref[p] for p in pairs]
    for p in pairs:
        mt = jnp.where(eye, p_c[:, sls[p]], 0.0) + gmat[p][:, :pair]
        nt = gmat[p][:, pair:] + hmat[p]
        st_ref[p] = _mm(mt, st0[p], _NN) + nt
    for p in pairs:
        rp_s = rts[p] + z[p][:, :pair]
        yv_s = z[p][:, pair:] + wv[p][pair:]
        rp = rp_s[:C] + rp_s[C:]
        yv = yv_s[:C] + yv_s[C:]
        y_ref[0, :, sls[p]] = _mm(rp, st0[p], _NN) + yv


def _wkv_scan(r, lw, k, v, a, b, *, head):
    bsz, seq, n = r.shape
    C = SCAN_CHUNK
    spec = pl.BlockSpec((1, C, n), lambda i, c: (i, c, 0))
    return pl.pallas_call(
        functools.partial(_scan_kernel, head=head),
        grid=(bsz, seq // C),
        in_specs=[spec] * 6,
        out_specs=spec,
        out_shape=jax.ShapeDtypeStruct((bsz, seq, n), F32),
        scratch_shapes=[pltpu.VMEM((n // (2 * head), 2 * head, 2 * head), F32)],
        compiler_params=_cparams(("arbitrary", "arbitrary"), 48),
        name="wkv_scan",
    )(r, lw, k, v, a, b)


def _post_kernel(y_ref, r_ref, k_ref, v_ref, g_ref, rk_ref, lnw_ref, lnb_ref, bd_ref, wpb_ref, gate_ref,
                 pa_ref, wout_ref, x_ref, n2_ref, x1_o, h2_o, *, head):
    bd = bd_ref[...]
    y = y_ref[...]
    inv = 1.0 / head
    mean = _head_sum(y, bd) * inv
    d = y - mean
    var = _head_sum(d * d, bd) * inv
    yn = d * lax.rsqrt(var + GN_EPS) * lnw_ref[...] + lnb_ref[...]
    r, k, v = (ref[...].astype(F32) for ref in (r_ref, k_ref, v_ref))
    bonus = _head_sum(r * k * rk_ref[...], bd) * v
    yb = ((yn + bonus) * g_ref[...].astype(F32)).astype(BF16)
    pb = jnp.dot(yb, wpb_ref[...], preferred_element_type=F32) * gate_ref[...].astype(F32)
    merged = (pa_ref[...].astype(F32) + pb).astype(BF16)
    x1 = x_ref[...] + jnp.dot(merged, wout_ref[...], preferred_element_type=F32)
    x1_o[...] = x1
    h2_o[...] = _rmsnorm(x1, n2_ref[...]).astype(h2_o.dtype)


def _rwkv_post(y, r, k, v, g, rk, lnw, lnb, bd, wpb, gates, pa, wout, x2, n2, *, head, tm):
    m, db = y.shape
    dm = x2.shape[1]
    row = lambda n, j=0: pl.BlockSpec((tm, n), lambda i: (i, j))
    vec = lambda n: pl.BlockSpec((1, n), lambda i: (0, 0))
    full = lambda a: pl.BlockSpec(a.shape, lambda i: (0, 0), pipeline_mode=pl.Buffered(1))
    return pl.pallas_call(
        functools.partial(_post_kernel, head=head),
        grid=(m // tm,),
        in_specs=[row(db), row(db), row(db), row(db), row(db), vec(db), vec(db), vec(db), full(bd), full(wpb),
                  row(dm, 1), row(dm), full(wout), row(dm), vec(dm)],
        out_specs=[row(dm), row(dm)],
        out_shape=[jax.ShapeDtypeStruct((m, dm), F32), jax.ShapeDtypeStruct((m, dm), BF16)],
        compiler_params=_cparams(("parallel",), 56),
        name="rwkv_post",
    )(y, r, k, v, g, rk, lnw, lnb, bd, wpb, gates, pa, wout, x2, n2)


def _ffn_kernel(h_ref, hp_ref, x1_ref, wg_ref, wv_ref, cwg_ref, cwv_ref, cbg_ref, cbv_ref, wd_ref, nf_ref,
                o_ref, hext_ref, ug_ref, uv_ref, acc_ref, *, seq, taps):
    tm = h_ref.shape[0]
    halo = hp_ref.shape[0]
    i = pl.program_id(0)
    j = pl.program_id(1)

    @pl.when(j == 0)
    def _():
        first = (i * tm) % seq == 0
        hext_ref[0:halo, :] = jnp.where(first, jnp.zeros_like(hp_ref), hp_ref[...])
        hext_ref[halo:, :] = h_ref[...]
        acc_ref[...] = jnp.zeros_like(acc_ref)

    hext = hext_ref[...]
    ug_ref[...] = jnp.dot(hext, wg_ref[...], preferred_element_type=F32)
    uv_ref[...] = jnp.dot(hext, wv_ref[...], preferred_element_type=F32)

    def conv(u_ref, cw_ref, cb_ref):
        out = cb_ref[...]
        for t in range(taps):
            off = halo - (taps - 1) + t
            out = out + cw_ref[t:t + 1, :] * u_ref[off:off + tm, :]
        return out

    act = (_gelu(conv(ug_ref, cwg_ref, cbg_ref)) * conv(uv_ref, cwv_ref, cbv_ref)).astype(BF16)
    acc_ref[...] += jnp.dot(act, wd_ref[...], preferred_element_type=F32)

    @pl.when(j == pl.num_programs(1) - 1)
    def _():
        o_ref[...] = _rmsnorm(x1_ref[...] + acc_ref[...], nf_ref[...])


def _conv_ffn(h2, x1, wup, cw, cb, wd, nf, *, seq, tm, tn):
    m, dm = h2.shape
    dff = wd.shape[0]
    nj = dff // tn
    taps = cw.shape[0]
    halo = SUBLANES_BF16
    blocks_per_tile = tm // halo
    return pl.pallas_call(
        functools.partial(_ffn_kernel, seq=seq, taps=taps),
        grid=(m // tm, nj),
        in_specs=[
            pl.BlockSpec((tm, dm), lambda i, j: (i, 0)),
            pl.BlockSpec((halo, dm), lambda i, j: (jnp.maximum(i * blocks_per_tile - 1, 0), 0)),
            pl.BlockSpec((tm, dm), lambda i, j: (i, 0)),
            pl.BlockSpec((dm, tn), lambda i, j: (0, j)),
            pl.BlockSpec((dm, tn), lambda i, j: (0, nj + j)),
            pl.BlockSpec((taps, tn), lambda i, j: (0, j)),
            pl.BlockSpec((taps, tn), lambda i, j: (0, nj + j)),
            pl.BlockSpec((1, tn), lambda i, j: (0, j)),
            pl.BlockSpec((1, tn), lambda i, j: (0, nj + j)),
            pl.BlockSpec((tn, dm), lambda i, j: (j, 0)),
            pl.BlockSpec((1, dm), lambda i, j: (0, 0)),
        ],
        out_specs=pl.BlockSpec((tm, dm), lambda i, j: (i, 0)),
        out_shape=jax.ShapeDtypeStruct((m, dm), F32),
        scratch_shapes=[
            pltpu.VMEM((tm + halo, dm), BF16),
            pltpu.VMEM((tm + halo, tn), F32),
            pltpu.VMEM((tm + halo, tn), F32),
            pltpu.VMEM((tm, dm), F32),
        ],
        compiler_params=_cparams(("parallel", "arbitrary"), 56),
        name="conv_ffn",
    )(h2, h2, x1, wup, wup, cw, cw, cb, cb, wd, nf)


def _pick_tile(n, want):
    t = min(want, n)
    while n % t:
        t //= 2
    return t


def _block_diag_ones(width, block):
    idx = jnp.arange(width) // block
    return (idx[:, None] == idx[None, :]).astype(BF16)


def _layer(x2, seq, p):
    m, dm = x2.shape
    da = p["gmlp_ln_w"].shape[0]
    db = p["rwkv_w0"].shape[0]
    n_heads, head = p["rwkv_rk"].shape
    lora_w = p["rwkv_w2"].shape[0]
    lora_a = p["rwkv_a2"].shape[0]
    lora_g = p["rwkv_g2"].shape[0]
    assert lora_w + lora_a == LANES and lora_g <= MXU_WIDTH
    d_b_in = 3 * db + lora_w + lora_a + lora_g
    w_in = p["w_in"]
    nf_pad = 3 * db + LANES + MXU_WIDTH
    tn_in = 2 * MXU_WIDTH
    nf_store = -(-nf_pad // tn_in) * tn_in
    w_all = jnp.concatenate([
        w_in[:, :2 * da],
        jnp.pad(w_in[:, 2 * da:2 * da + d_b_in], ((0, 0), (0, nf_store - d_b_in))),
        w_in[:, 2 * da + d_b_in:]], axis=1).astype(BF16)
    uv, feat, gates = _inproj(x2, p["norm1_g"].reshape(1, dm), w_all, widths=(2 * da, nf_store, 2 * dm),
                              tm=_pick_tile(m, 1024), tn=tn_in)

    pa = _gmlp(uv, gates, p["gmlp_ln_w"].reshape(1, da), p["gmlp_ln_b"].reshape(1, da), p["gmlp_ws"],
               p["gmlp_bs"].T, p["w_proj_a"].astype(BF16), tm=_pick_tile(seq, 512))

    mu = jnp.pad(p["mu_b"], (0, nf_pad - d_b_in)).reshape(1, nf_pad)
    zeros = jnp.zeros((lora_w, db), F32)
    w01 = jnp.concatenate([jnp.concatenate([p["rwkv_w2"], zeros], 1),
                           jnp.concatenate([jnp.zeros((lora_a, db), F32), p["rwkv_a2"]], 1)], 0).astype(BF16)
    g2p = jnp.pad(p["rwkv_g2"], ((0, MXU_WIDTH - lora_g), (0, 0))).astype(BF16)
    bd = _block_diag_ones(MXU_WIDTH, head)
    vecb = lambda a: a.reshape(1, db)
    r, lw, k, v, av, bv, g = _rwkv_prep(
        feat, mu, vecb(p["rwkv_w0"]), vecb(p["rwkv_a0"]), vecb(p["rwkv_kk"]), vecb(p["rwkv_ka"]), w01, g2p, bd,
        seq=seq, tm=_pick_tile(seq, 256), lora_wa=lora_w)
    bsz = m // seq
    to3 = lambda t: t.reshape(bsz, seq, db)
    y = _wkv_scan(to3(r), to3(lw), to3(k), to3(v), to3(av), to3(bv), head=head).reshape(m, db)
    x1, h2 = _rwkv_post(y, r, k, v, g, vecb(p["rwkv_rk"]), vecb(p["rwkv_ln_w"]), vecb(p["rwkv_ln_b"]), bd,
                        p["w_proj_b"].astype(BF16), gates, pa, p["w_out"].astype(BF16), x2,
                        p["norm2_g"].reshape(1, dm), head=head, tm=_pick_tile(seq, 256))
    return x1, h2


def kernel(x, norm1_g, w_in, mu_b, rwkv_w0, rwkv_w2, rwkv_a0, rwkv_a2, rwkv_g2, rwkv_kk, rwkv_ka, rwkv_rk, rwkv_ln_w, rwkv_ln_b, gmlp_ln_w, gmlp_ln_b, gmlp_ws, gmlp_bs, w_proj_a, w_proj_b, w_out, norm2_g, w_up, conv_w, conv_b, w_down, norm_f_g):
    bsz, seq, dm = x.shape
    depth = w_in.shape[0]
    assert depth == 1, "the fused FFN applies the final rmsnorm; one layer is supported"
    x2 = x.reshape(bsz * seq, dm)
    l = 0
    p = dict(norm1_g=norm1_g[l], w_in=w_in[l], mu_b=mu_b[l], rwkv_w0=rwkv_w0[l], rwkv_w2=rwkv_w2[l],
             rwkv_a0=rwkv_a0[l], rwkv_a2=rwkv_a2[l], rwkv_g2=rwkv_g2[l], rwkv_kk=rwkv_kk[l], rwkv_ka=rwkv_ka[l],
             rwkv_rk=rwkv_rk[l], rwkv_ln_w=rwkv_ln_w[l], rwkv_ln_b=rwkv_ln_b[l], gmlp_ln_w=gmlp_ln_w[l],
             gmlp_ln_b=gmlp_ln_b[l], gmlp_ws=gmlp_ws[l], gmlp_bs=gmlp_bs[l], w_proj_a=w_proj_a[l],
             w_proj_b=w_proj_b[l], w_out=w_out[l], norm2_g=norm2_g[l])
    x1, h2 = _layer(x2, seq, p)
    dff = w_down.shape[1]
    out = _conv_ffn(h2, x1, w_up[l].astype(BF16), conv_w[l], conv_b[l].reshape(1, -1), w_down[l].astype(BF16),
                    norm_f_g.reshape(1, dm), seq=seq, tm=_pick_tile(seq, 512), tn=_pick_tile(dff // 1, 512))
    return out.reshape(bsz, seq, dm)
```

```python
import functools

import jax
import jax.numpy as jnp
from jax import lax
from jax.experimental import pallas as pl
from jax.experimental.pallas import tpu as pltpu

F32 = jnp.float32
BF16 = jnp.bfloat16

RMS_EPS = 1e-6
LN_EPS = 1e-5
GN_EPS = 64e-5
KK_NORM_FLOOR = 1e-12
EXP_NEG_HALF = 0.6065306597126334

LANES = 128
MXU_WIDTH = 256
SUBLANES_F32 = 8
SUBLANES_BF16 = 16

SCAN_CHUNK = 64


def _cparams(semantics, vmem_mb):
    return pltpu.CompilerParams(dimension_semantics=semantics, vmem_limit_bytes=vmem_mb * 1024 * 1024)


def _rmsnorm(x, g):
    ms = jnp.mean(x * x, axis=-1, keepdims=True)
    return x * lax.rsqrt(ms + RMS_EPS) * g


def _gelu(x):
    return 0.5 * x * (1.0 + lax.erf(x * (0.5 ** 0.5)))


def _sigmoid(x):
    return 0.5 * (jnp.tanh(0.5 * x) + 1.0)


def _bdot(a, b):
    return jnp.dot(a.astype(BF16), b.astype(BF16), preferred_element_type=F32)


def _inproj_kernel(x_ref, g_ref, wuv_ref, wfeat_ref, wgate_ref, uv_ref, feat_ref, gate_ref, h_ref, *, n_uv, n_feat):
    j = pl.program_id(1)

    @pl.when(j == 0)
    def _():
        h_ref[...] = _rmsnorm(x_ref[...], g_ref[...]).astype(h_ref.dtype)

    def project(w_ref):
        return jnp.dot(h_ref[...], w_ref[...], preferred_element_type=F32)

    @pl.when(j < n_uv)
    def _():
        uv_ref[...] = _gelu(project(wuv_ref)).astype(uv_ref.dtype)

    @pl.when((j >= n_uv) & (j < n_uv + n_feat))
    def _():
        feat_ref[...] = project(wfeat_ref)

    @pl.when(j >= n_uv + n_feat)
    def _():
        gate_ref[...] = _sigmoid(project(wgate_ref)).astype(gate_ref.dtype)


def _inproj(x2, g, w_uv, w_feat, w_gate, *, tm, tn):
    m, d = x2.shape
    widths = tuple(w.shape[1] for w in (w_uv, w_feat, w_gate))
    n_uv, n_feat, n_gate = (wd // tn for wd in widths)
    assert all(wd % tn == 0 for wd in widths)
    col_uv = lambda j: jnp.minimum(j, n_uv - 1)
    col_feat = lambda j: jnp.clip(j - n_uv, 0, n_feat - 1)
    col_gate = lambda j: jnp.clip(j - n_uv - n_feat, 0, n_gate - 1)
    return pl.pallas_call(
        functools.partial(_inproj_kernel, n_uv=n_uv, n_feat=n_feat),
        grid=(m // tm, n_uv + n_feat + n_gate),
        in_specs=[
            pl.BlockSpec((tm, d), lambda i, j: (i, 0)),
            pl.BlockSpec((1, d), lambda i, j: (0, 0)),
            pl.BlockSpec((d, tn), lambda i, j: (0, col_uv(j))),
            pl.BlockSpec((d, tn), lambda i, j: (0, col_feat(j))),
            pl.BlockSpec((d, tn), lambda i, j: (0, col_gate(j))),
        ],
        out_specs=[
            pl.BlockSpec((tm, tn), lambda i, j: (i, col_uv(j))),
            pl.BlockSpec((tm, tn), lambda i, j: (i, col_feat(j))),
            pl.BlockSpec((tm, tn), lambda i, j: (i, col_gate(j))),
        ],
        out_shape=[
            jax.ShapeDtypeStruct((m, widths[0]), BF16),
            jax.ShapeDtypeStruct((m, widths[1]), F32),
            jax.ShapeDtypeStruct((m, widths[2]), BF16),
        ],
        scratch_shapes=[pltpu.VMEM((tm, d), BF16)],
        compiler_params=_cparams(("parallel", "arbitrary"), 48),
        name="inproj",
    )(x2, g, w_uv, w_feat, w_gate)


def _gmlp_kernel(u_ref, v_ref, lnw_ref, lnb_ref, ws_ref, bst_ref, wpa_ref, gate_ref, o_ref, ya_ref, *, chunk):
    tm, da = v_ref.shape
    n_groups = ws_ref.shape[0]
    gw = da // n_groups
    v = v_ref[...].astype(F32)
    mu = jnp.mean(v, axis=-1, keepdims=True)
    d = v - mu
    var = jnp.mean(d * d, axis=-1, keepdims=True)
    vn = (d * lax.rsqrt(var + LN_EPS) * lnw_ref[...] + lnb_ref[...]).astype(BF16)
    row = lax.broadcasted_iota(jnp.int32, (chunk, chunk), 0)
    col = lax.broadcasted_iota(jnp.int32, (chunk, chunk), 1)
    causal = row >= col
    for g in range(n_groups):
        wm = jnp.where(causal, ws_ref[g], 0.0).astype(BF16)
        bias = bst_ref[:, g:g + 1]
        for c in range(tm // chunk):
            rs = slice(c * chunk, (c + 1) * chunk)
            cs = slice(g * gw, (g + 1) * gw)
            mixed = jnp.dot(wm, vn[rs, cs], preferred_element_type=F32) + bias
            ya_ref[rs, cs] = (u_ref[rs, cs].astype(F32) * mixed).astype(BF16)
    pa = jnp.dot(ya_ref[...], wpa_ref[...], preferred_element_type=F32)
    o_ref[...] = (pa * gate_ref[...].astype(F32)).astype(o_ref.dtype)


def _gmlp(uv, gates, lnw, lnb, ws, bst, wpa, *, tm):
    m = uv.shape[0]
    da = uv.shape[1] // 2
    dm = wpa.shape[1]
    chunk = ws.shape[1]
    return pl.pallas_call(
        functools.partial(_gmlp_kernel, chunk=chunk),
        grid=(m // tm,),
        in_specs=[
            pl.BlockSpec((tm, da), lambda i: (i, 0)),
            pl.BlockSpec((tm, da), lambda i: (i, 1)),
            pl.BlockSpec((1, da), lambda i: (0, 0)),
            pl.BlockSpec((1, da), lambda i: (0, 0)),
            pl.BlockSpec(ws.shape, lambda i: (0, 0, 0)),
            pl.BlockSpec(bst.shape, lambda i: (0, 0)),
            pl.BlockSpec(wpa.shape, lambda i: (0, 0)),
            pl.BlockSpec((tm, dm), lambda i: (i, 0)),
        ],
        out_specs=pl.BlockSpec((tm, dm), lambda i: (i, 0)),
        out_shape=jax.ShapeDtypeStruct((m, dm), BF16),
        scratch_shapes=[pltpu.VMEM((tm, da), BF16)],
        compiler_params=_cparams(("parallel",), 48),
        name="gmlp",
    )(uv, uv, lnw, lnb, ws, bst, wpa, gates)


def _head_sum(x, bd):
    hi = x.astype(BF16)
    lo = (x - hi.astype(F32)).astype(BF16)
    outs = []
    for j in range(x.shape[1] // MXU_WIDTH):
        sl = slice(j * MXU_WIDTH, (j + 1) * MXU_WIDTH)
        outs.append(jnp.dot(hi[:, sl], bd, preferred_element_type=F32)
                    + jnp.dot(lo[:, sl], bd, preferred_element_type=F32))
    return jnp.concatenate(outs, axis=1)


def _prep_kernel(feat_ref, prev_ref, mu_ref, w0_ref, a0_ref, kk_ref, ka_ref, w01_ref, g2_ref, bd_ref,
                 r_o, lw_o, k_o, v_o, a_o, b_o, g_o, *, seq, db, lora_wa):
    tm = feat_ref.shape[0]
    i = pl.program_id(0)
    feat = feat_ref[...]
    first = (i * tm) % seq == 0
    prev = jnp.where(first, 0.0, prev_ref[SUBLANES_F32 - 1:SUBLANES_F32, :])
    rows = lax.broadcasted_iota(jnp.int32, (tm, 1), 0)
    shifted = jnp.where(rows == 0, prev, pltpu.roll(feat, 1, 0))
    fm = feat + (shifted - feat) * mu_ref[...]
    r = fm[:, 0:db]
    k = fm[:, db:2 * db]
    v = fm[:, 2 * db:3 * db]
    l0 = fm[:, 3 * db:3 * db + LANES]
    l1 = fm[:, 3 * db + LANES:]
    lane = lax.broadcasted_iota(jnp.int32, l0.shape, 1)
    x0 = jnp.where(lane < lora_wa, jnp.tanh(l0), l0)
    wa = _bdot(x0, w01_ref[...])
    g = _bdot(_sigmoid(l1), g2_ref[...])
    w_raw = w0_ref[...] + wa[:, :db]
    a = _sigmoid(a0_ref[...] + wa[:, db:])
    lw = -EXP_NEG_HALF * _sigmoid(w_raw)
    kkv = k * kk_ref[...]
    kkn = kkv * lax.rsqrt(jnp.maximum(_head_sum(kkv * kkv, bd_ref[...]), KK_NORM_FLOOR ** 2))
    kp = k * (1.0 + (a - 1.0) * ka_ref[...])
    lw_o[...] = lw
    for o_ref, val in ((r_o, r), (k_o, kp), (v_o, v), (a_o, -kkn), (b_o, kkn * a), (g_o, g)):
        o_ref[...] = val.astype(o_ref.dtype)


def _rwkv_prep(feat, mu, w0, a0, kk, ka, w01, g2p, bd, *, seq, tm, lora_wa):
    m = feat.shape[0]
    nf = mu.shape[1]
    db = w0.shape[1]
    vec = lambda n: pl.BlockSpec((1, n), lambda i: (0, 0))
    full = lambda a: pl.BlockSpec(a.shape, lambda i: (0, 0))
    out_spec = pl.BlockSpec((tm, db), lambda i: (i, 0))
    blocks_per_tile = tm // SUBLANES_F32
    return pl.pallas_call(
        functools.partial(_prep_kernel, seq=seq, db=db, lora_wa=lora_wa),
        grid=(m // tm,),
        in_specs=[
            pl.BlockSpec((tm, nf), lambda i: (i, 0)),
            pl.BlockSpec((SUBLANES_F32, nf), lambda i: (jnp.maximum(i * blocks_per_tile - 1, 0), 0)),
            vec(nf), vec(db), vec(db), vec(db), vec(db), full(w01), full(g2p), full(bd),
        ],
        out_specs=[out_spec] * 7,
        out_shape=[jax.ShapeDtypeStruct((m, db), F32 if i == 1 else BF16) for i in range(7)],
        compiler_params=_cparams(("parallel",), 48),
        name="rwkv_prep",
    )(feat, feat, mu, w0, a0, kk, ka, w01, g2p, bd)


def _mm(a, b, dims):
    return lax.dot_general(a.astype(BF16), b.astype(BF16), (dims, ((), ())), preferred_element_type=F32)


_NN = ((1,), (0,))
_NT = ((1,), (1,))
_TN = ((0,), (0,))


def _scan_kernel(r_ref, lw_ref, k_ref, v_ref, a_ref, b_ref, y_ref, st_ref, *, head):
    c = pl.program_id(1)
    C = lw_ref.shape[1]
    n = lw_ref.shape[2]
    pair = 2 * head
    assert pair == LANES and 2 * C == LANES

    @pl.when(c == 0)
    def _():
        st_ref[...] = jnp.zeros_like(st_ref)

    lw = lw_ref[0]
    rc = lax.broadcasted_iota(jnp.int32, (C, C), 0)
    cc = lax.broadcasted_iota(jnp.int32, (C, C), 1)
    tri = (rc >= cc).astype(F32)
    L = lax.dot_general(tri, lw, (_NN, ((), ())), precision=lax.Precision.HIGHEST, preferred_element_type=F32)
    LC = L[C - 1:C, :]
    p_in = jnp.exp(L)
    p_ex = jnp.exp(L - lw)
    p_inv = jnp.exp(-L)
    p_end = jnp.exp(LC - L)
    p_c = jnp.exp(LC)
    r, k, v, a, b = (ref[0].astype(F32) for ref in (r_ref, k_ref, v_ref, a_ref, b_ref))
    rt = r * p_in
    at = a * p_ex
    bt = b * p_inv
    kt = k * p_inv
    bh = b * p_end
    kh = k * p_end

    lane = lax.broadcasted_iota(jnp.int32, (C, pair), 1)
    m0 = lane < head
    ri = lax.broadcasted_iota(jnp.int32, (pair, pair), 0)
    ci = lax.broadcasted_iota(jnp.int32, (pair, pair), 1)
    same = (ri >= C) == (ci >= C)
    strict = same & (ri > ci)
    incl = same & (ri >= ci)
    eye = ri == ci
    eye_f = eye.astype(F32)

    def stack(x):
        return jnp.concatenate([jnp.where(m0, x, 0.0), jnp.where(m0, 0.0, x)], axis=0)

    pairs = range(n // pair)
    sls = [slice(p * pair, (p + 1) * pair) for p in pairs]
    ats, rts, bts, kts, vs, bhs, khs = ([stack(x[:, sl]) for sl in sls] for x in (at, rt, bt, kt, v, bh, kh))
    amat = [_mm(jnp.concatenate([ats[p], rts[p]], 0), jnp.concatenate([bts[p], kts[p]], 0), _NT) for p in pairs]
    aab = [jnp.where(strict, amat[p][:pair, :pair], 0.0) for p in pairs]
    aak = [jnp.where(strict, amat[p][:pair, pair:], 0.0) for p in pairs]
    arb = [jnp.where(incl, amat[p][pair:, :pair], 0.0) for p in pairs]
    ark = [jnp.where(incl, amat[p][pair:, pair:], 0.0) for p in pairs]
    t_inv = [eye_f + aab[p] for p in pairs]
    aj = [_mm(aab[p], aab[p], _NN) for p in pairs]
    w1 = [_mm(aak[p], vs[p], _NN) for p in pairs]
    n_factors = C.bit_length() - 1
    for _ in range(n_factors - 2):
        res = [_mm(aj[p], jnp.concatenate([t_inv[p], aj[p]], 1), _NN) for p in pairs]
        t_inv = [t_inv[p] + res[p][:, :pair] for p in pairs]
        aj = [res[p][:, pair:] for p in pairs]
    t_inv = [t_inv[p] + _mm(aj[p], t_inv[p], _NN) for p in pairs]
    x = [_mm(t_inv[p], jnp.concatenate([ats[p], w1[p]], 1), _NN) for p in pairs]
    zero = jnp.zeros((pair, pair), F32)
    xv = [jnp.concatenate([x[p], jnp.concatenate([zero, vs[p]], 1)], 0) for p in pairs]
    z = [_mm(jnp.concatenate([arb[p], ark[p]], 1), xv[p], _NN) for p in pairs]
    gmat = [_mm(jnp.concatenate([bhs[p], khs[p]], 0), xv[p], _TN) for p in pairs]
    st0 = [st_ref[p] for p in pairs]
    for p in pairs:
        mt = jnp.where(eye, p_c[:, sls[p]], 0.0) + gmat[p][:, :pair]
        st_ref[p] = _mm(mt, st0[p], _NN) + gmat[p][:, pair:]
    for p in pairs:
        rp_s = rts[p] + z[p][:, :pair]
        yv_s = z[p][:, pair:]
        rp = rp_s[:C] + rp_s[C:]
        yv = yv_s[:C] + yv_s[C:]
        y_ref[0, :, sls[p]] = _mm(rp, st0[p], _NN) + yv


def _wkv_scan(r, lw, k, v, a, b, *, head):
    bsz, seq, n = r.shape
    C = SCAN_CHUNK
    spec = pl.BlockSpec((1, C, n), lambda i, c: (i, c, 0))
    return pl.pallas_call(
        functools.partial(_scan_kernel, head=head),
        grid=(bsz, seq // C),
        in_specs=[spec] * 6,
        out_specs=spec,
        out_shape=jax.ShapeDtypeStruct((bsz, seq, n), F32),
        scratch_shapes=[pltpu.VMEM((n // (2 * head), 2 * head, 2 * head), F32)],
        compiler_params=_cparams(("arbitrary", "arbitrary"), 48),
        name="wkv_scan",
    )(r, lw, k, v, a, b)


def _post_kernel(y_ref, r_ref, k_ref, v_ref, g_ref, rk_ref, lnw_ref, lnb_ref, bd_ref, wpb_ref, gate_ref,
                 pa_ref, wout_ref, x_ref, n2_ref, x1_o, h2_o, *, head):
    bd = bd_ref[...]
    y = y_ref[...]
    inv = 1.0 / head
    mean = _head_sum(y, bd) * inv
    d = y - mean
    var = _head_sum(d * d, bd) * inv
    yn = d * lax.rsqrt(var + GN_EPS) * lnw_ref[...] + lnb_ref[...]
    r, k, v = (ref[...].astype(F32) for ref in (r_ref, k_ref, v_ref))
    bonus = _head_sum(r * k * rk_ref[...], bd) * v
    yb = ((yn + bonus) * g_ref[...].astype(F32)).astype(BF16)
    pb = jnp.dot(yb, wpb_ref[...], preferred_element_type=F32) * gate_ref[...].astype(F32)
    merged = (pa_ref[...].astype(F32) + pb).astype(BF16)
    x1 = x_ref[...] + jnp.dot(merged, wout_ref[...], preferred_element_type=F32)
    x1_o[...] = x1
    h2_o[...] = _rmsnorm(x1, n2_ref[...]).astype(h2_o.dtype)


def _rwkv_post(y, r, k, v, g, rk, lnw, lnb, bd, wpb, gates, pa, wout, x2, n2, *, head, tm):
    m, db = y.shape
    dm = x2.shape[1]
    row = lambda n, j=0: pl.BlockSpec((tm, n), lambda i: (i, j))
    vec = lambda n: pl.BlockSpec((1, n), lambda i: (0, 0))
    full = lambda a: pl.BlockSpec(a.shape, lambda i: (0, 0), pipeline_mode=pl.Buffered(1))
    return pl.pallas_call(
        functools.partial(_post_kernel, head=head),
        grid=(m // tm,),
        in_specs=[row(db), row(db), row(db), row(db), row(db), vec(db), vec(db), vec(db), full(bd), full(wpb),
                  row(dm, 1), row(dm), full(wout), row(dm), vec(dm)],
        out_specs=[row(dm), row(dm)],
        out_shape=[jax.ShapeDtypeStruct((m, dm), F32), jax.ShapeDtypeStruct((m, dm), BF16)],
        compiler_params=_cparams(("parallel",), 56),
        name="rwkv_post",
    )(y, r, k, v, g, rk, lnw, lnb, bd, wpb, gates, pa, wout, x2, n2)


def _ffn_kernel(h_ref, hp_ref, x1_ref, wg_ref, wv_ref, cwg_ref, cwv_ref, cbg_ref, cbv_ref, wd_ref, nf_ref,
                o_ref, hext_ref, ug_ref, uv_ref, acc_ref, *, seq, taps):
    tm = h_ref.shape[0]
    halo = hp_ref.shape[0]
    i = pl.program_id(0)
    j = pl.program_id(1)

    @pl.when(j == 0)
    def _():
        first = (i * tm) % seq == 0
        hext_ref[0:halo, :] = jnp.where(first, jnp.zeros_like(hp_ref), hp_ref[...])
        hext_ref[halo:, :] = h_ref[...]
        acc_ref[...] = jnp.zeros_like(acc_ref)

    hext = hext_ref[...]
    ug_ref[...] = jnp.dot(hext, wg_ref[...], preferred_element_type=F32)
    uv_ref[...] = jnp.dot(hext, wv_ref[...], preferred_element_type=F32)

    def conv(u_ref, cw_ref, cb_ref):
        out = cb_ref[...]
        for t in range(taps):
            off = halo - (taps - 1) + t
            out = out + cw_ref[t:t + 1, :] * u_ref[off:off + tm, :]
        return out

    act = (_gelu(conv(ug_ref, cwg_ref, cbg_ref)) * conv(uv_ref, cwv_ref, cbv_ref)).astype(BF16)
    acc_ref[...] += jnp.dot(act, wd_ref[...], preferred_element_type=F32)

    @pl.when(j == pl.num_programs(1) - 1)
    def _():
        o_ref[...] = _rmsnorm(x1_ref[...] + acc_ref[...], nf_ref[...])


def _conv_ffn(h2, x1, wup, cw, cb, wd, nf, *, seq, tm, tn):
    m, dm = h2.shape
    dff = wd.shape[0]
    nj = dff // tn
    taps = cw.shape[0]
    halo = SUBLANES_BF16
    blocks_per_tile = tm // halo
    return pl.pallas_call(
        functools.partial(_ffn_kernel, seq=seq, taps=taps),
        grid=(m // tm, nj),
        in_specs=[
            pl.BlockSpec((tm, dm), lambda i, j: (i, 0)),
            pl.BlockSpec((halo, dm), lambda i, j: (jnp.maximum(i * blocks_per_tile - 1, 0), 0)),
            pl.BlockSpec((tm, dm), lambda i, j: (i, 0)),
            pl.BlockSpec((dm, tn), lambda i, j: (0, j)),
            pl.BlockSpec((dm, tn), lambda i, j: (0, nj + j)),
            pl.BlockSpec((taps, tn), lambda i, j: (0, j)),
            pl.BlockSpec((taps, tn), lambda i, j: (0, nj + j)),
            pl.BlockSpec((1, tn), lambda i, j: (0, j)),
            pl.BlockSpec((1, tn), lambda i, j: (0, nj + j)),
            pl.BlockSpec((tn, dm), lambda i, j: (j, 0)),
            pl.BlockSpec((1, dm), lambda i, j: (0, 0)),
        ],
        out_specs=pl.BlockSpec((tm, dm), lambda i, j: (i, 0)),
        out_shape=jax.ShapeDtypeStruct((m, dm), F32),
        scratch_shapes=[
            pltpu.VMEM((tm + halo, dm), BF16),
            pltpu.VMEM((tm + halo, tn), F32),
            pltpu.VMEM((tm + halo, tn), F32),
            pltpu.VMEM((tm, dm), F32),
        ],
        compiler_params=_cparams(("parallel", "arbitrary"), 56),
        name="conv_ffn",
    )(h2, h2, x1, wup, wup, cw, cw, cb, cb, wd, nf)


def _pick_tile(n, want):
    t = min(want, n)
    while n % t:
        t //= 2
    return t


def _block_diag_ones(width, block):
    idx = jnp.arange(width) // block
    return (idx[:, None] == idx[None, :]).astype(BF16)


def _layer(x2, seq, p):
    m, dm = x2.shape
    da = p["gmlp_ln_w"].shape[0]
    db = p["rwkv_w0"].shape[0]
    n_heads, head = p["rwkv_rk"].shape
    lora_w = p["rwkv_w2"].shape[0]
    lora_a = p["rwkv_a2"].shape[0]
    lora_g = p["rwkv_g2"].shape[0]
    assert lora_w + lora_a == LANES and lora_g <= MXU_WIDTH
    d_b_in = 3 * db + lora_w + lora_a + lora_g
    w_in = p["w_in"]
    nf_pad = 3 * db + LANES + MXU_WIDTH
    tn_in = 2 * MXU_WIDTH
    nf_store = -(-nf_pad // tn_in) * tn_in
    w_uv = w_in[:, :2 * da].astype(BF16)
    w_feat = jnp.pad(w_in[:, 2 * da:2 * da + d_b_in].astype(BF16), ((0, 0), (0, nf_store - d_b_in)))
    w_gate = w_in[:, 2 * da + d_b_in:].astype(BF16)
    uv, feat, gates = _inproj(x2, p["norm1_g"].reshape(1, dm), w_uv, w_feat, w_gate, tm=_pick_tile(m, 1024), tn=tn_in)

    pa = _gmlp(uv, gates, p["gmlp_ln_w"].reshape(1, da), p["gmlp_ln_b"].reshape(1, da), p["gmlp_ws"],
               p["gmlp_bs"].T, p["w_proj_a"].astype(BF16), tm=_pick_tile(seq, 512))

    mu = jnp.pad(p["mu_b"], (0, nf_pad - d_b_in)).reshape(1, nf_pad)
    zeros = jnp.zeros((lora_w, db), F32)
    w01 = jnp.concatenate([jnp.concatenate([p["rwkv_w2"], zeros], 1),
                           jnp.concatenate([jnp.zeros((lora_a, db), F32), p["rwkv_a2"]], 1)], 0).astype(BF16)
    g2p = jnp.pad(p["rwkv_g2"], ((0, MXU_WIDTH - lora_g), (0, 0))).astype(BF16)
    bd = _block_diag_ones(MXU_WIDTH, head)
    vecb = lambda a: a.reshape(1, db)
    r, lw, k, v, av, bv, g = _rwkv_prep(
        feat, mu, vecb(p["rwkv_w0"]), vecb(p["rwkv_a0"]), vecb(p["rwkv_kk"]), vecb(p["rwkv_ka"]), w01, g2p, bd,
        seq=seq, tm=_pick_tile(seq, 256), lora_wa=lora_w)
    bsz = m // seq
    to3 = lambda t: t.reshape(bsz, seq, db)
    y = _wkv_scan(to3(r), to3(lw), to3(k), to3(v), to3(av), to3(bv), head=head).reshape(m, db)
    x1, h2 = _rwkv_post(y, r, k, v, g, vecb(p["rwkv_rk"]), vecb(p["rwkv_ln_w"]), vecb(p["rwkv_ln_b"]), bd,
                        p["w_proj_b"].astype(BF16), gates, pa, p["w_out"].astype(BF16), x2,
                        p["norm2_g"].reshape(1, dm), head=head, tm=_pick_tile(seq, 256))
    return x1, h2


def kernel(x, norm1_g, w_in, mu_b, rwkv_w0, rwkv_w2, rwkv_a0, rwkv_a2, rwkv_g2, rwkv_kk, rwkv_ka, rwkv_rk, rwkv_ln_w, rwkv_ln_b, gmlp_ln_w, gmlp_ln_b, gmlp_ws, gmlp_bs, w_proj_a, w_proj_b, w_out, norm2_g, w_up, conv_w, conv_b, w_down, norm_f_g):
    bsz, seq, dm = x.shape
    depth = w_in.shape[0]
    assert depth == 1, "the fused FFN applies the final rmsnorm; one layer is supported"
    x2 = x.reshape(bsz * seq, dm)
    l = 0
    p = dict(norm1_g=norm1_g[l], w_in=w_in[l], mu_b=mu_b[l], rwkv_w0=rwkv_w0[l], rwkv_w2=rwkv_w2[l],
             rwkv_a0=rwkv_a0[l], rwkv_a2=rwkv_a2[l], rwkv_g2=rwkv_g2[l], rwkv_kk=rwkv_kk[l], rwkv_ka=rwkv_ka[l],
             rwkv_rk=rwkv_rk[l], rwkv_ln_w=rwkv_ln_w[l], rwkv_ln_b=rwkv_ln_b[l], gmlp_ln_w=gmlp_ln_w[l],
             gmlp_ln_b=gmlp_ln_b[l], gmlp_ws=gmlp_ws[l], gmlp_bs=gmlp_bs[l], w_proj_a=w_proj_a[l],
             w_proj_b=w_proj_b[l], w_out=w_out[l], norm2_g=norm2_g[l])
    x1, h2 = _layer(x2, seq, p)
    dff = w_down.shape[1]
    out = _conv_ffn(h2, x1, w_up[l].astype(BF16), conv_w[l], conv_b[l].reshape(1, -1), w_down[l].astype(BF16),
                    norm_f_g.reshape(1, dm), seq=seq, tm=_pick_tile(seq, 512), tn=_pick_tile(dff // 1, 512))
    return out.reshape(bsz, seq, dm)
```

```python
import functools

import jax
import jax.numpy as jnp
from jax import lax
from jax.experimental import pallas as pl
from jax.experimental.pallas import tpu as pltpu

F32 = jnp.float32
BF16 = jnp.bfloat16

RMS_EPS = 1e-6
LN_EPS = 1e-5
GN_EPS = 64e-5
KK_NORM_FLOOR = 1e-12
EXP_NEG_HALF = 0.6065306597126334

LANES = 128
MXU_WIDTH = 256
SUBLANES_F32 = 8
SUBLANES_BF16 = 16

SCAN_CHUNK = 64


def _cparams(semantics, vmem_mb):
    return pltpu.CompilerParams(dimension_semantics=semantics, vmem_limit_bytes=vmem_mb * 1024 * 1024)


def _rmsnorm(x, g):
    ms = jnp.mean(x * x, axis=-1, keepdims=True)
    return x * lax.rsqrt(ms + RMS_EPS) * g


def _gelu(x):
    return 0.5 * x * (1.0 + lax.erf(x * (0.5 ** 0.5)))


def _sigmoid(x):
    return 0.5 * (jnp.tanh(0.5 * x) + 1.0)


def _bdot(a, b):
    return jnp.dot(a.astype(BF16), b.astype(BF16), preferred_element_type=F32)


def _inproj_kernel(x_ref, g_ref, wuv_ref, wfeat_ref, wgate_ref, uv_ref, feat_ref, gate_ref, h_ref, *, n_uv, n_feat):
    j = pl.program_id(1)

    @pl.when(j == 0)
    def _():
        h_ref[...] = _rmsnorm(x_ref[...], g_ref[...]).astype(h_ref.dtype)

    def project(w_ref):
        return jnp.dot(h_ref[...], w_ref[...], preferred_element_type=F32)

    @pl.when(j < n_uv)
    def _():
        uv_ref[...] = _gelu(project(wuv_ref)).astype(uv_ref.dtype)

    @pl.when((j >= n_uv) & (j < n_uv + n_feat))
    def _():
        feat_ref[...] = project(wfeat_ref).astype(feat_ref.dtype)

    @pl.when(j >= n_uv + n_feat)
    def _():
        gate_ref[...] = _sigmoid(project(wgate_ref)).astype(gate_ref.dtype)


def _column_tiles(w, tn):
    d, n = w.shape
    assert n % tn == 0
    return w.reshape(d, n // tn, tn).transpose(1, 0, 2)


def _inproj(x2, g, w_uv, w_feat, w_gate, *, tm, tn):
    m, d = x2.shape
    widths = tuple(w.shape[1] for w in (w_uv, w_feat, w_gate))
    n_uv, n_feat, n_gate = (wd // tn for wd in widths)
    w_uv, w_feat, w_gate = (_column_tiles(w, tn) for w in (w_uv, w_feat, w_gate))
    col_uv = lambda j: jnp.minimum(j, n_uv - 1)
    col_feat = lambda j: jnp.clip(j - n_uv, 0, n_feat - 1)
    col_gate = lambda j: jnp.clip(j - n_uv - n_feat, 0, n_gate - 1)
    return pl.pallas_call(
        functools.partial(_inproj_kernel, n_uv=n_uv, n_feat=n_feat),
        grid=(m // tm, n_uv + n_feat + n_gate),
        in_specs=[
            pl.BlockSpec((tm, d), lambda i, j: (i, 0)),
            pl.BlockSpec((1, d), lambda i, j: (0, 0)),
            pl.BlockSpec((None, d, tn), lambda i, j: (col_uv(j), 0, 0)),
            pl.BlockSpec((None, d, tn), lambda i, j: (col_feat(j), 0, 0)),
            pl.BlockSpec((None, d, tn), lambda i, j: (col_gate(j), 0, 0)),
        ],
        out_specs=[
            pl.BlockSpec((tm, tn), lambda i, j: (i, col_uv(j))),
            pl.BlockSpec((tm, tn), lambda i, j: (i, col_feat(j))),
            pl.BlockSpec((tm, tn), lambda i, j: (i, col_gate(j))),
        ],
        out_shape=[
            jax.ShapeDtypeStruct((m, widths[0]), BF16),
            jax.ShapeDtypeStruct((m, widths[1]), BF16),
            jax.ShapeDtypeStruct((m, widths[2]), BF16),
        ],
        scratch_shapes=[pltpu.VMEM((tm, d), BF16)],
        compiler_params=_cparams(("parallel", "arbitrary"), 48),
        name="inproj",
    )(x2, g, w_uv, w_feat, w_gate)


def _gmlp_kernel(u_ref, v_ref, lnw_ref, lnb_ref, ws_ref, bst_ref, wpa_ref, gate_ref, o_ref, ya_ref, *, chunk):
    tm, da = v_ref.shape
    n_groups = ws_ref.shape[0]
    gw = da // n_groups
    v = v_ref[...].astype(F32)
    mu = jnp.mean(v, axis=-1, keepdims=True)
    d = v - mu
    var = jnp.mean(d * d, axis=-1, keepdims=True)
    vn = (d * lax.rsqrt(var + LN_EPS) * lnw_ref[...] + lnb_ref[...]).astype(BF16)
    row = lax.broadcasted_iota(jnp.int32, (chunk, chunk), 0)
    col = lax.broadcasted_iota(jnp.int32, (chunk, chunk), 1)
    causal = row >= col
    for g in range(n_groups):
        wm = jnp.where(causal, ws_ref[g], 0.0).astype(BF16)
        bias = bst_ref[:, g:g + 1]
        for c in range(tm // chunk):
            rs = slice(c * chunk, (c + 1) * chunk)
            cs = slice(g * gw, (g + 1) * gw)
            mixed = jnp.dot(wm, vn[rs, cs], preferred_element_type=F32) + bias
            ya_ref[rs, cs] = (u_ref[rs, cs].astype(F32) * mixed).astype(BF16)
    pa = jnp.dot(ya_ref[...], wpa_ref[...], preferred_element_type=F32)
    o_ref[...] = (pa * gate_ref[...].astype(F32)).astype(o_ref.dtype)


def _gmlp(uv, gates, lnw, lnb, ws, bst, wpa, *, tm):
    m = uv.shape[0]
    da = uv.shape[1] // 2
    dm = wpa.shape[1]
    chunk = ws.shape[1]
    return pl.pallas_call(
        functools.partial(_gmlp_kernel, chunk=chunk),
        grid=(m // tm,),
        in_specs=[
            pl.BlockSpec((tm, da), lambda i: (i, 0)),
            pl.BlockSpec((tm, da), lambda i: (i, 1)),
            pl.BlockSpec((1, da), lambda i: (0, 0)),
            pl.BlockSpec((1, da), lambda i: (0, 0)),
            pl.BlockSpec(ws.shape, lambda i: (0, 0, 0)),
            pl.BlockSpec(bst.shape, lambda i: (0, 0)),
            pl.BlockSpec(wpa.shape, lambda i: (0, 0)),
            pl.BlockSpec((tm, dm), lambda i: (i, 0)),
        ],
        out_specs=pl.BlockSpec((tm, dm), lambda i: (i, 0)),
        out_shape=jax.ShapeDtypeStruct((m, dm), BF16),
        scratch_shapes=[pltpu.VMEM((tm, da), BF16)],
        compiler_params=_cparams(("parallel",), 48),
        name="gmlp",
    )(uv, uv, lnw, lnb, ws, bst, wpa, gates)


def _head_sum(x, bd):
    hi = x.astype(BF16)
    lo = (x - hi.astype(F32)).astype(BF16)
    outs = []
    for j in range(x.shape[1] // MXU_WIDTH):
        sl = slice(j * MXU_WIDTH, (j + 1) * MXU_WIDTH)
        outs.append(jnp.dot(hi[:, sl], bd, preferred_element_type=F32)
                    + jnp.dot(lo[:, sl], bd, preferred_element_type=F32))
    return jnp.concatenate(outs, axis=1)


def _prep_kernel(feat_ref, prev_ref, mu_ref, w0_ref, a0_ref, kk_ref, ka_ref, w01_ref, g2_ref, bd_ref,
                 r_o, lw_o, k_o, v_o, a_o, b_o, g_o, *, seq, db, lora_wa):
    tm = feat_ref.shape[0]
    i = pl.program_id(0)
    feat = feat_ref[...].astype(F32)
    first = (i * tm) % seq == 0
    halo = prev_ref.shape[0]
    prev = jnp.where(first, 0.0, prev_ref[halo - 1:halo, :].astype(F32))
    rows = lax.broadcasted_iota(jnp.int32, (tm, 1), 0)
    shifted = jnp.where(rows == 0, prev, pltpu.roll(feat, 1, 0))
    fm = feat + (shifted - feat) * mu_ref[...]
    r = fm[:, 0:db]
    k = fm[:, db:2 * db]
    v = fm[:, 2 * db:3 * db]
    l0 = fm[:, 3 * db:3 * db + LANES]
    l1 = fm[:, 3 * db + LANES:]
    lane = lax.broadcasted_iota(jnp.int32, l0.shape, 1)
    x0 = jnp.where(lane < lora_wa, jnp.tanh(l0), l0)
    wa = _bdot(x0, w01_ref[...])
    g = _bdot(_sigmoid(l1), g2_ref[...])
    w_raw = w0_ref[...] + wa[:, :db]
    a = _sigmoid(a0_ref[...] + wa[:, db:])
    lw = -EXP_NEG_HALF * _sigmoid(w_raw)
    kkv = k * kk_ref[...]
    kkn = kkv * lax.rsqrt(jnp.maximum(_head_sum(kkv * kkv, bd_ref[...]), KK_NORM_FLOOR ** 2))
    kp = k * (1.0 + (a - 1.0) * ka_ref[...])
    lw_o[...] = lw
    for o_ref, val in ((r_o, r), (k_o, kp), (v_o, v), (a_o, -kkn), (b_o, kkn * a), (g_o, g)):
        o_ref[...] = val.astype(o_ref.dtype)


def _rwkv_prep(feat, mu, w0, a0, kk, ka, w01, g2p, bd, *, seq, tm, lora_wa):
    m = feat.shape[0]
    nf = mu.shape[1]
    db = w0.shape[1]
    vec = lambda n: pl.BlockSpec((1, n), lambda i: (0, 0))
    full = lambda a: pl.BlockSpec(a.shape, lambda i: (0, 0))
    out_spec = pl.BlockSpec((tm, db), lambda i: (i, 0))
    halo = SUBLANES_BF16 if feat.dtype == BF16 else SUBLANES_F32
    blocks_per_tile = tm // halo
    return pl.pallas_call(
        functools.partial(_prep_kernel, seq=seq, db=db, lora_wa=lora_wa),
        grid=(m // tm,),
        in_specs=[
            pl.BlockSpec((tm, nf), lambda i: (i, 0)),
            pl.BlockSpec((halo, nf), lambda i: (jnp.maximum(i * blocks_per_tile - 1, 0), 0)),
            vec(nf), vec(db), vec(db), vec(db), vec(db), full(w01), full(g2p), full(bd),
        ],
        out_specs=[out_spec] * 7,
        out_shape=[jax.ShapeDtypeStruct((m, db), F32 if i == 1 else BF16) for i in range(7)],
        compiler_params=_cparams(("parallel",), 48),
        name="rwkv_prep",
    )(feat, feat, mu, w0, a0, kk, ka, w01, g2p, bd)


def _mm(a, b, dims):
    return lax.dot_general(a.astype(BF16), b.astype(BF16), (dims, ((), ())), preferred_element_type=F32)


_NN = ((1,), (0,))
_NT = ((1,), (1,))
_TN = ((0,), (0,))


def _scan_kernel(r_ref, lw_ref, k_ref, v_ref, a_ref, b_ref, y_ref, st_ref, *, head):
    c = pl.program_id(1)
    C = lw_ref.shape[1]
    n = lw_ref.shape[2]
    pair = 2 * head
    assert pair == LANES and 2 * C == LANES

    @pl.when(c == 0)
    def _():
        st_ref[...] = jnp.zeros_like(st_ref)

    lw = lw_ref[0]
    rc = lax.broadcasted_iota(jnp.int32, (C, C), 0)
    cc = lax.broadcasted_iota(jnp.int32, (C, C), 1)
    tri = (rc >= cc).astype(F32)
    L = lax.dot_general(tri, lw, (_NN, ((), ())), precision=lax.Precision.HIGHEST, preferred_element_type=F32)
    LC = L[C - 1:C, :]
    p_in = jnp.exp(L)
    p_ex = jnp.exp(L - lw)
    p_inv = jnp.exp(-L)
    p_end = jnp.exp(LC - L)
    p_c = jnp.exp(LC)
    r, k, v, a, b = (ref[0].astype(F32) for ref in (r_ref, k_ref, v_ref, a_ref, b_ref))
    rt = r * p_in
    at = a * p_ex
    bt = b * p_inv
    kt = k * p_inv
    bh = b * p_end
    kh = k * p_end

    lane = lax.broadcasted_iota(jnp.int32, (C, pair), 1)
    m0 = lane < head
    ri = lax.broadcasted_iota(jnp.int32, (pair, pair), 0)
    ci = lax.broadcasted_iota(jnp.int32, (pair, pair), 1)
    same = (ri >= C) == (ci >= C)
    strict = same & (ri > ci)
    incl = same & (ri >= ci)
    eye = ri == ci
    eye_f = eye.astype(F32)

    def stack(x):
        return jnp.concatenate([jnp.where(m0, x, 0.0), jnp.where(m0, 0.0, x)], axis=0)

    pairs = range(n // pair)
    sls = [slice(p * pair, (p + 1) * pair) for p in pairs]
    ats, rts, bts, kts, vs, bhs, khs = ([stack(x[:, sl]) for sl in sls] for x in (at, rt, bt, kt, v, bh, kh))
    amat = [_mm(jnp.concatenate([ats[p], rts[p]], 0), jnp.concatenate([bts[p], kts[p]], 0), _NT) for p in pairs]
    aab = [jnp.where(strict, amat[p][:pair, :pair], 0.0) for p in pairs]
    aak = [jnp.where(strict, amat[p][:pair, pair:], 0.0) for p in pairs]
    arb = [jnp.where(incl, amat[p][pair:, :pair], 0.0) for p in pairs]
    ark = [jnp.where(incl, amat[p][pair:, pair:], 0.0) for p in pairs]
    t_inv = [eye_f + aab[p] for p in pairs]
    aj = [_mm(aab[p], aab[p], _NN) for p in pairs]
    w1 = [_mm(aak[p], vs[p], _NN) for p in pairs]
    n_factors = C.bit_length() - 1
    for _ in range(n_factors - 2):
        res = [_mm(aj[p], jnp.concatenate([t_inv[p], aj[p]], 1), _NN) for p in pairs]
        t_inv = [t_inv[p] + res[p][:, :pair] for p in pairs]
        aj = [res[p][:, pair:] for p in pairs]
    t_inv = [t_inv[p] + _mm(aj[p], t_inv[p], _NN) for p in pairs]
    x = [_mm(t_inv[p], jnp.concatenate([ats[p], w1[p]], 1), _NN) for p in pairs]
    zero = jnp.zeros((pair, pair), F32)
    xv = [jnp.concatenate([x[p], jnp.concatenate([zero, vs[p]], 1)], 0) for p in pairs]
    z = [_mm(jnp.concatenate([arb[p], ark[p]], 1), xv[p], _NN) for p in pairs]
    gmat = [_mm(jnp.concatenate([bhs[p], khs[p]], 0), xv[p], _TN) for p in pairs]
    st0 = [st_ref[p] for p in pairs]
    for p in pairs:
        mt = jnp.where(eye, p_c[:, sls[p]], 0.0) + gmat[p][:, :pair]
        st_ref[p] = _mm(mt, st0[p], _NN) + gmat[p][:, pair:]
    for p in pairs:
        rp_s = rts[p] + z[p][:, :pair]
        yv_s = z[p][:, pair:]
        rp = rp_s[:C] + rp_s[C:]
        yv = yv_s[:C] + yv_s[C:]
        y_ref[0, :, sls[p]] = _mm(rp, st0[p], _NN) + yv


def _wkv_scan(r, lw, k, v, a, b, *, head):
    bsz, seq, n = r.shape
    C = SCAN_CHUNK
    spec = pl.BlockSpec((1, C, n), lambda i, c: (i, c, 0))
    return pl.pallas_call(
        functools.partial(_scan_kernel, head=head),
        grid=(bsz, seq // C),
        in_specs=[spec] * 6,
        out_specs=spec,
        out_shape=jax.ShapeDtypeStruct((bsz, seq, n), F32),
        scratch_shapes=[pltpu.VMEM((n // (2 * head), 2 * head, 2 * head), F32)],
        compiler_params=_cparams(("arbitrary", "arbitrary"), 48),
        name="wkv_scan",
    )(r, lw, k, v, a, b)


def _post_kernel(y_ref, r_ref, k_ref, v_ref, g_ref, rk_ref, lnw_ref, lnb_ref, bd_ref, wpb_ref, gate_ref,
                 pa_ref, wout_ref, x_ref, n2_ref, x1_o, h2_o, *, head):
    bd = bd_ref[...]
    y = y_ref[...]
    inv = 1.0 / head
    mean = _head_sum(y, bd) * inv
    d = y - mean
    var = _head_sum(d * d, bd) * inv
    yn = d * lax.rsqrt(var + GN_EPS) * lnw_ref[...] + lnb_ref[...]
    r, k, v = (ref[...].astype(F32) for ref in (r_ref, k_ref, v_ref))
    bonus = _head_sum(r * k * rk_ref[...], bd) * v
    yb = ((yn + bonus) * g_ref[...].astype(F32)).astype(BF16)
    pb = jnp.dot(yb, wpb_ref[...], preferred_element_type=F32) * gate_ref[...].astype(F32)
    merged = (pa_ref[...].astype(F32) + pb).astype(BF16)
    x1 = x_ref[...] + jnp.dot(merged, wout_ref[...], preferred_element_type=F32)
    x1_o[...] = x1
    h2_o[...] = _rmsnorm(x1, n2_ref[...]).astype(h2_o.dtype)


def _rwkv_post(y, r, k, v, g, rk, lnw, lnb, bd, wpb, gates, pa, wout, x2, n2, *, head, tm):
    m, db = y.shape
    dm = x2.shape[1]
    row = lambda n, j=0: pl.BlockSpec((tm, n), lambda i: (i, j))
    vec = lambda n: pl.BlockSpec((1, n), lambda i: (0, 0))
    full = lambda a: pl.BlockSpec(a.shape, lambda i: (0, 0), pipeline_mode=pl.Buffered(1))
    return pl.pallas_call(
        functools.partial(_post_kernel, head=head),
        grid=(m // tm,),
        in_specs=[row(db), row(db), row(db), row(db), row(db), vec(db), vec(db), vec(db), full(bd), full(wpb),
                  row(dm, 1), row(dm), full(wout), row(dm), vec(dm)],
        out_specs=[row(dm), row(dm)],
        out_shape=[jax.ShapeDtypeStruct((m, dm), F32), jax.ShapeDtypeStruct((m, dm), BF16)],
        compiler_params=_cparams(("parallel",), 56),
        name="rwkv_post",
    )(y, r, k, v, g, rk, lnw, lnb, bd, wpb, gates, pa, wout, x2, n2)


def _ffn_kernel(h_ref, hp_ref, x1_ref, wg_ref, wv_ref, cwg_ref, cwv_ref, cbg_ref, cbv_ref, wd_ref, nf_ref,
                o_ref, hext_ref, ug_ref, uv_ref, acc_ref, *, seq, taps):
    tm = h_ref.shape[0]
    halo = hp_ref.shape[0]
    i = pl.program_id(0)
    j = pl.program_id(1)

    @pl.when(j == 0)
    def _():
        first = (i * tm) % seq == 0
        hext_ref[0:halo, :] = jnp.where(first, jnp.zeros_like(hp_ref), hp_ref[...])
        hext_ref[halo:, :] = h_ref[...]
        acc_ref[...] = jnp.zeros_like(acc_ref)

    hext = hext_ref[...]
    ug_ref[...] = jnp.dot(hext, wg_ref[...], preferred_element_type=F32)
    uv_ref[...] = jnp.dot(hext, wv_ref[...], preferred_element_type=F32)

    def conv(u_ref, cw_ref, cb_ref):
        out = cb_ref[...]
        for t in range(taps):
            off = halo - (taps - 1) + t
            out = out + cw_ref[t:t + 1, :] * u_ref[off:off + tm, :]
        return out

    act = (_gelu(conv(ug_ref, cwg_ref, cbg_ref)) * conv(uv_ref, cwv_ref, cbv_ref)).astype(BF16)
    acc_ref[...] += jnp.dot(act, wd_ref[...], preferred_element_type=F32)

    @pl.when(j == pl.num_programs(1) - 1)
    def _():
        o_ref[...] = _rmsnorm(x1_ref[...] + acc_ref[...], nf_ref[...])


def _conv_ffn(h2, x1, wup, cw, cb, wd, nf, *, seq, tm, tn):
    m, dm = h2.shape
    dff = wd.shape[0]
    nj = dff // tn
    taps = cw.shape[0]
    halo = SUBLANES_BF16
    blocks_per_tile = tm // halo
    wup_t = _column_tiles(wup, tn)
    return pl.pallas_call(
        functools.partial(_ffn_kernel, seq=seq, taps=taps),
        grid=(m // tm, nj),
        in_specs=[
            pl.BlockSpec((tm, dm), lambda i, j: (i, 0)),
            pl.BlockSpec((halo, dm), lambda i, j: (jnp.maximum(i * blocks_per_tile - 1, 0), 0)),
            pl.BlockSpec((tm, dm), lambda i, j: (i, 0)),
            pl.BlockSpec((None, dm, tn), lambda i, j: (j, 0, 0)),
            pl.BlockSpec((None, dm, tn), lambda i, j: (nj + j, 0, 0)),
            pl.BlockSpec((taps, tn), lambda i, j: (0, j)),
            pl.BlockSpec((taps, tn), lambda i, j: (0, nj + j)),
            pl.BlockSpec((1, tn), lambda i, j: (0, j)),
            pl.BlockSpec((1, tn), lambda i, j: (0, nj + j)),
            pl.BlockSpec((tn, dm), lambda i, j: (j, 0)),
            pl.BlockSpec((1, dm), lambda i, j: (0, 0)),
        ],
        out_specs=pl.BlockSpec((tm, dm), lambda i, j: (i, 0)),
        out_shape=jax.ShapeDtypeStruct((m, dm), F32),
        scratch_shapes=[
            pltpu.VMEM((tm + halo, dm), BF16),
            pltpu.VMEM((tm + halo, tn), F32),
            pltpu.VMEM((tm + halo, tn), F32),
            pltpu.VMEM((tm, dm), F32),
        ],
        compiler_params=_cparams(("parallel", "arbitrary"), 56),
        name="conv_ffn",
    )(h2, h2, x1, wup_t, wup_t, cw, cw, cb, cb, wd, nf)


def _pick_tile(n, want):
    t = min(want, n)
    while n % t:
        t //= 2
    return t


def _block_diag_ones(width, block):
    idx = jnp.arange(width) // block
    return (idx[:, None] == idx[None, :]).astype(BF16)


def _layer(x2, seq, p):
    m, dm = x2.shape
    da = p["gmlp_ln_w"].shape[0]
    db = p["rwkv_w0"].shape[0]
    n_heads, head = p["rwkv_rk"].shape
    lora_w = p["rwkv_w2"].shape[0]
    lora_a = p["rwkv_a2"].shape[0]
    lora_g = p["rwkv_g2"].shape[0]
    assert lora_w + lora_a == LANES and lora_g <= MXU_WIDTH
    d_b_in = 3 * db + lora_w + lora_a + lora_g
    w_in = p["w_in"]
    nf_pad = 3 * db + LANES + MXU_WIDTH
    tn_in = 2 * MXU_WIDTH
    nf_store = -(-nf_pad // tn_in) * tn_in
    w_uv = w_in[:, :2 * da].astype(BF16)
    w_feat = jnp.pad(w_in[:, 2 * da:2 * da + d_b_in].astype(BF16), ((0, 0), (0, nf_store - d_b_in)))
    w_gate = w_in[:, 2 * da + d_b_in:].astype(BF16)
    uv, feat, gates = _inproj(x2, p["norm1_g"].reshape(1, dm), w_uv, w_feat, w_gate, tm=_pick_tile(m, 1024), tn=tn_in)

    pa = _gmlp(uv, gates, p["gmlp_ln_w"].reshape(1, da), p["gmlp_ln_b"].reshape(1, da), p["gmlp_ws"],
               p["gmlp_bs"].T, p["w_proj_a"].astype(BF16), tm=_pick_tile(seq, 512))

    mu = jnp.pad(p["mu_b"], (0, nf_pad - d_b_in)).reshape(1, nf_pad)
    zeros = jnp.zeros((lora_w, db), F32)
    w01 = jnp.concatenate([jnp.concatenate([p["rwkv_w2"], zeros], 1),
                           jnp.concatenate([jnp.zeros((lora_a, db), F32), p["rwkv_a2"]], 1)], 0).astype(BF16)
    g2p = jnp.pad(p["rwkv_g2"], ((0, MXU_WIDTH - lora_g), (0, 0))).astype(BF16)
    bd = _block_diag_ones(MXU_WIDTH, head)
    vecb = lambda a: a.reshape(1, db)
    r, lw, k, v, av, bv, g = _rwkv_prep(
        feat, mu, vecb(p["rwkv_w0"]), vecb(p["rwkv_a0"]), vecb(p["rwkv_kk"]), vecb(p["rwkv_ka"]), w01, g2p, bd,
        seq=seq, tm=_pick_tile(seq, 256), lora_wa=lora_w)
    bsz = m // seq
    to3 = lambda t: t.reshape(bsz, seq, db)
    y = _wkv_scan(to3(r), to3(lw), to3(k), to3(v), to3(av), to3(bv), head=head).reshape(m, db)
    x1, h2 = _rwkv_post(y, r, k, v, g, vecb(p["rwkv_rk"]), vecb(p["rwkv_ln_w"]), vecb(p["rwkv_ln_b"]), bd,
                        p["w_proj_b"].astype(BF16), gates, pa, p["w_out"].astype(BF16), x2,
                        p["norm2_g"].reshape(1, dm), head=head, tm=_pick_tile(seq, 256))
    return x1, h2


def kernel(x, norm1_g, w_in, mu_b, rwkv_w0, rwkv_w2, rwkv_a0, rwkv_a2, rwkv_g2, rwkv_kk, rwkv_ka, rwkv_rk, rwkv_ln_w, rwkv_ln_b, gmlp_ln_w, gmlp_ln_b, gmlp_ws, gmlp_bs, w_proj_a, w_proj_b, w_out, norm2_g, w_up, conv_w, conv_b, w_down, norm_f_g):
    bsz, seq, dm = x.shape
    depth = w_in.shape[0]
    assert depth == 1, "the fused FFN applies the final rmsnorm; one layer is supported"
    x2 = x.reshape(bsz * seq, dm)
    l = 0
    p = dict(norm1_g=norm1_g[l], w_in=w_in[l], mu_b=mu_b[l], rwkv_w0=rwkv_w0[l], rwkv_w2=rwkv_w2[l],
             rwkv_a0=rwkv_a0[l], rwkv_a2=rwkv_a2[l], rwkv_g2=rwkv_g2[l], rwkv_kk=rwkv_kk[l], rwkv_ka=rwkv_ka[l],
             rwkv_rk=rwkv_rk[l], rwkv_ln_w=rwkv_ln_w[l], rwkv_ln_b=rwkv_ln_b[l], gmlp_ln_w=gmlp_ln_w[l],
             gmlp_ln_b=gmlp_ln_b[l], gmlp_ws=gmlp_ws[l], gmlp_bs=gmlp_bs[l], w_proj_a=w_proj_a[l],
             w_proj_b=w_proj_b[l], w_out=w_out[l], norm2_g=norm2_g[l])
    x1, h2 = _layer(x2, seq, p)
    dff = w_down.shape[1]
    out = _conv_ffn(h2, x1, w_up[l].astype(BF16), conv_w[l], conv_b[l].reshape(1, -1), w_down[l].astype(BF16),
                    norm_f_g.reshape(1, dm), seq=seq, tm=_pick_tile(seq, 512), tn=_pick_tile(dff // 1, 512))
    return out.reshape(bsz, seq, dm)
```

```python
import functools

import jax
import jax.numpy as jnp
from jax import lax
from jax.experimental import pallas as pl
from jax.experimental.pallas import tpu as pltpu

F32 = jnp.float32
BF16 = jnp.bfloat16

RMS_EPS = 1e-6
LN_EPS = 1e-5
GN_EPS = 64e-5
KK_NORM_FLOOR = 1e-12
EXP_NEG_HALF = 0.6065306597126334

LANES = 128
MXU_WIDTH = 256
SUBLANES_F32 = 8
SUBLANES_BF16 = 16

SCAN_CHUNK = 64
SCAN_CHUNKS_PER_STEP = 2


def _cparams(semantics, vmem_mb):
    return pltpu.CompilerParams(dimension_semantics=semantics, vmem_limit_bytes=vmem_mb * 1024 * 1024)


def _rmsnorm(x, g):
    ms = jnp.mean(x * x, axis=-1, keepdims=True)
    return x * lax.rsqrt(ms + RMS_EPS) * g


def _gelu(x):
    return 0.5 * x * (1.0 + lax.erf(x * (0.5 ** 0.5)))


def _sigmoid(x):
    return 0.5 * (jnp.tanh(0.5 * x) + 1.0)


def _bdot(a, b):
    return jnp.dot(a.astype(BF16), b.astype(BF16), preferred_element_type=F32)


def _inproj_kernel(x_ref, g_ref, w_ref, uv_ref, feat_ref, gate_ref, h_ref, *, n_uv, n_feat):
    j = pl.program_id(1)

    @pl.when(j == 0)
    def _():
        h_ref[...] = _rmsnorm(x_ref[...], g_ref[...]).astype(h_ref.dtype)

    def project():
        return jnp.dot(h_ref[...], w_ref[...], preferred_element_type=F32)

    @pl.when(j < n_uv)
    def _():
        uv_ref[...] = _gelu(project()).astype(uv_ref.dtype)

    @pl.when((j >= n_uv) & (j < n_uv + n_feat))
    def _():
        feat_ref[...] = project().astype(feat_ref.dtype)

    @pl.when(j >= n_uv + n_feat)
    def _():
        gate_ref[...] = _sigmoid(project()).astype(gate_ref.dtype)


def _inproj(x2, g, w, *, widths, tm, tn):
    m, d = x2.shape
    n_uv, n_feat, n_gate = (wd // tn for wd in widths)
    assert all(wd % tn == 0 for wd in widths) and sum(widths) == w.shape[1]
    return pl.pallas_call(
        functools.partial(_inproj_kernel, n_uv=n_uv, n_feat=n_feat),
        grid=(m // tm, n_uv + n_feat + n_gate),
        in_specs=[
            pl.BlockSpec((tm, d), lambda i, j: (i, 0)),
            pl.BlockSpec((1, d), lambda i, j: (0, 0)),
            pl.BlockSpec((d, tn), lambda i, j: (0, j)),
        ],
        out_specs=[
            pl.BlockSpec((tm, tn), lambda i, j: (i, jnp.minimum(j, n_uv - 1))),
            pl.BlockSpec((tm, tn), lambda i, j: (i, jnp.clip(j - n_uv, 0, n_feat - 1))),
            pl.BlockSpec((tm, tn), lambda i, j: (i, jnp.clip(j - n_uv - n_feat, 0, n_gate - 1))),
        ],
        out_shape=[
            jax.ShapeDtypeStruct((m, widths[0]), BF16),
            jax.ShapeDtypeStruct((m, widths[1]), BF16),
            jax.ShapeDtypeStruct((m, widths[2]), BF16),
        ],
        scratch_shapes=[pltpu.VMEM((tm, d), BF16)],
        compiler_params=_cparams(("parallel", "arbitrary"), 48),
        name="inproj",
    )(x2, g, w)


def _gmlp_kernel(u_ref, v_ref, lnw_ref, lnb_ref, ws_ref, bst_ref, wpa_ref, gate_ref, o_ref, ya_ref, *, chunk):
    tm, da = v_ref.shape
    n_groups = ws_ref.shape[0]
    gw = da // n_groups
    v = v_ref[...].astype(F32)
    mu = jnp.mean(v, axis=-1, keepdims=True)
    d = v - mu
    var = jnp.mean(d * d, axis=-1, keepdims=True)
    vn = (d * lax.rsqrt(var + LN_EPS) * lnw_ref[...] + lnb_ref[...]).astype(BF16)
    row = lax.broadcasted_iota(jnp.int32, (chunk, chunk), 0)
    col = lax.broadcasted_iota(jnp.int32, (chunk, chunk), 1)
    causal = row >= col
    for g in range(n_groups):
        wm = jnp.where(causal, ws_ref[g], 0.0).astype(BF16)
        bias = bst_ref[:, g:g + 1]
        for c in range(tm // chunk):
            rs = slice(c * chunk, (c + 1) * chunk)
            cs = slice(g * gw, (g + 1) * gw)
            mixed = jnp.dot(wm, vn[rs, cs], preferred_element_type=F32) + bias
            ya_ref[rs, cs] = (u_ref[rs, cs].astype(F32) * mixed).astype(BF16)
    pa = jnp.dot(ya_ref[...], wpa_ref[...], preferred_element_type=F32)
    o_ref[...] = (pa * gate_ref[...].astype(F32)).astype(o_ref.dtype)


def _gmlp(uv, gates, lnw, lnb, ws, bst, wpa, *, tm):
    m = uv.shape[0]
    da = uv.shape[1] // 2
    dm = wpa.shape[1]
    chunk = ws.shape[1]
    return pl.pallas_call(
        functools.partial(_gmlp_kernel, chunk=chunk),
        grid=(m // tm,),
        in_specs=[
            pl.BlockSpec((tm, da), lambda i: (i, 0)),
            pl.BlockSpec((tm, da), lambda i: (i, 1)),
            pl.BlockSpec((1, da), lambda i: (0, 0)),
            pl.BlockSpec((1, da), lambda i: (0, 0)),
            pl.BlockSpec(ws.shape, lambda i: (0, 0, 0)),
            pl.BlockSpec(bst.shape, lambda i: (0, 0)),
            pl.BlockSpec(wpa.shape, lambda i: (0, 0)),
            pl.BlockSpec((tm, dm), lambda i: (i, 0)),
        ],
        out_specs=pl.BlockSpec((tm, dm), lambda i: (i, 0)),
        out_shape=jax.ShapeDtypeStruct((m, dm), BF16),
        scratch_shapes=[pltpu.VMEM((tm, da), BF16)],
        compiler_params=_cparams(("parallel",), 48),
        name="gmlp",
    )(uv, uv, lnw, lnb, ws, bst, wpa, gates)


def _head_sum(x, bd):
    xb = x.astype(BF16)
    outs = [jnp.dot(xb[:, j:j + MXU_WIDTH], bd, preferred_element_type=F32) for j in range(0, x.shape[1], MXU_WIDTH)]
    return jnp.concatenate(outs, axis=1)


def _prep_kernel(feat_ref, prev_ref, mu_ref, w0_ref, a0_ref, kk_ref, ka_ref, w01_ref, g2_ref, bd_ref,
                 r_o, lw_o, k_o, v_o, a_o, b_o, g_o, *, seq, db, lora_wa):
    tm = feat_ref.shape[0]
    i = pl.program_id(0)
    feat = feat_ref[...].astype(F32)
    first = (i * tm) % seq == 0
    halo = prev_ref.shape[0]
    prev = jnp.where(first, 0.0, prev_ref[halo - 1:halo, :].astype(F32))
    rows = lax.broadcasted_iota(jnp.int32, (tm, 1), 0)
    shifted = jnp.where(rows == 0, prev, pltpu.roll(feat, 1, 0))
    fm = feat + (shifted - feat) * mu_ref[...]
    r = fm[:, 0:db]
    k = fm[:, db:2 * db]
    v = fm[:, 2 * db:3 * db]
    l0 = fm[:, 3 * db:3 * db + LANES]
    l1 = fm[:, 3 * db + LANES:]
    lane = lax.broadcasted_iota(jnp.int32, l0.shape, 1)
    x0 = jnp.where(lane < lora_wa, jnp.tanh(l0), l0)
    wa = _bdot(x0, w01_ref[...])
    g = _bdot(_sigmoid(l1), g2_ref[...])
    w_raw = w0_ref[...] + wa[:, :db]
    a = _sigmoid(a0_ref[...] + wa[:, db:])
    lw = -EXP_NEG_HALF * _sigmoid(w_raw)
    kkv = k * kk_ref[...]
    kkn = kkv * lax.rsqrt(jnp.maximum(_head_sum(kkv * kkv, bd_ref[...]), KK_NORM_FLOOR ** 2))
    kp = k * (1.0 + (a - 1.0) * ka_ref[...])
    lw_o[...] = lw
    for o_ref, val in ((r_o, r), (k_o, kp), (v_o, v), (a_o, -kkn), (b_o, kkn * a), (g_o, g)):
        o_ref[...] = val.astype(o_ref.dtype)


def _rwkv_prep(feat, mu, w0, a0, kk, ka, w01, g2p, bd, *, seq, tm, lora_wa):
    m = feat.shape[0]
    nf = mu.shape[1]
    db = w0.shape[1]
    vec = lambda n: pl.BlockSpec((1, n), lambda i: (0, 0))
    full = lambda a: pl.BlockSpec(a.shape, lambda i: (0, 0))
    out_spec = pl.BlockSpec((tm, db), lambda i: (i, 0))
    halo = SUBLANES_BF16 if feat.dtype == BF16 else SUBLANES_F32
    blocks_per_tile = tm // halo
    return pl.pallas_call(
        functools.partial(_prep_kernel, seq=seq, db=db, lora_wa=lora_wa),
        grid=(m // tm,),
        in_specs=[
            pl.BlockSpec((tm, nf), lambda i: (i, 0)),
            pl.BlockSpec((halo, nf), lambda i: (jnp.maximum(i * blocks_per_tile - 1, 0), 0)),
            vec(nf), vec(db), vec(db), vec(db), vec(db), full(w01), full(g2p), full(bd),
        ],
        out_specs=[out_spec] * 7,
        out_shape=[jax.ShapeDtypeStruct((m, db), F32 if i == 1 else BF16) for i in range(7)],
        compiler_params=_cparams(("parallel",), 48),
        name="rwkv_prep",
    )(feat, feat, mu, w0, a0, kk, ka, w01, g2p, bd)


def _mm(a, b, dims):
    return lax.dot_general(a.astype(BF16), b.astype(BF16), (dims, ((), ())), preferred_element_type=F32)


_NN = ((1,), (0,))
_NT = ((1,), (1,))
_TN = ((0,), (0,))


def _scan_kernel(r_ref, lw_ref, k_ref, v_ref, a_ref, b_ref, y_ref, st_ref, *, head, chunk):
    c = pl.program_id(1)
    C = chunk
    n = lw_ref.shape[2]
    pair = 2 * head
    assert pair == LANES and 2 * C == LANES
    pairs = range(n // pair)
    sls = [slice(p * pair, (p + 1) * pair) for p in pairs]

    @pl.when(c == 0)
    def _():
        st_ref[...] = jnp.zeros_like(st_ref)

    rc = lax.broadcasted_iota(jnp.int32, (C, C), 0)
    cc = lax.broadcasted_iota(jnp.int32, (C, C), 1)
    tri = (rc >= cc).astype(F32)
    lane = lax.broadcasted_iota(jnp.int32, (C, pair), 1)
    m0 = lane < head
    ri = lax.broadcasted_iota(jnp.int32, (pair, pair), 0)
    ci = lax.broadcasted_iota(jnp.int32, (pair, pair), 1)
    same = (ri >= C) == (ci >= C)
    strict = same & (ri > ci)
    incl = same & (ri >= ci)
    eye = ri == ci
    eye_f = eye.astype(F32)
    zero = jnp.zeros((pair, pair), F32)

    def stack(x):
        return jnp.concatenate([jnp.where(m0, x, 0.0), jnp.where(m0, 0.0, x)], axis=0)

    def one_chunk(rows, st0):
        lw = lw_ref[0, rows, :]
        L = lax.dot_general(tri, lw, (_NN, ((), ())), precision=lax.Precision.HIGHEST, preferred_element_type=F32)
        LC = L[C - 1:C, :]
        p_in = jnp.exp(L)
        p_ex = jnp.exp(L - lw)
        p_inv = jnp.exp(-L)
        p_end = jnp.exp(LC - L)
        p_c = jnp.exp(LC)
        r, k, v, a, b = (ref[0, rows, :].astype(F32) for ref in (r_ref, k_ref, v_ref, a_ref, b_ref))
        rt = r * p_in
        at = a * p_ex
        bt = b * p_inv
        kt = k * p_inv
        bh = b * p_end
        kh = k * p_end
        ats, rts, bts, kts, vs, bhs, khs = ([stack(x[:, sl]) for sl in sls] for x in (at, rt, bt, kt, v, bh, kh))
        amat = [_mm(jnp.concatenate([ats[p], rts[p]], 0), jnp.concatenate([bts[p], kts[p]], 0), _NT) for p in pairs]
        aab = [jnp.where(strict, amat[p][:pair, :pair], 0.0) for p in pairs]
        aak = [jnp.where(strict, amat[p][:pair, pair:], 0.0) for p in pairs]
        arb = [jnp.where(incl, amat[p][pair:, :pair], 0.0) for p in pairs]
        ark = [jnp.where(incl, amat[p][pair:, pair:], 0.0) for p in pairs]
        t_inv = [eye_f + aab[p] for p in pairs]
        aj = [_mm(aab[p], aab[p], _NN) for p in pairs]
        w1 = [_mm(aak[p], vs[p], _NN) for p in pairs]
        n_factors = C.bit_length() - 1
        for _ in range(n_factors - 2):
            res = [_mm(aj[p], jnp.concatenate([t_inv[p], aj[p]], 1), _NN) for p in pairs]
            t_inv = [t_inv[p] + res[p][:, :pair] for p in pairs]
            aj = [res[p][:, pair:] for p in pairs]
        t_inv = [t_inv[p] + _mm(aj[p], t_inv[p], _NN) for p in pairs]
        x = [_mm(t_inv[p], jnp.concatenate([ats[p], w1[p]], 1), _NN) for p in pairs]
        xv = [jnp.concatenate([x[p], jnp.concatenate([zero, vs[p]], 1)], 0) for p in pairs]
        z = [_mm(jnp.concatenate([arb[p], ark[p]], 1), xv[p], _NN) for p in pairs]
        gmat = [_mm(jnp.concatenate([bhs[p], khs[p]], 0), xv[p], _TN) for p in pairs]
        st1 = []
        for p in pairs:
            mt = jnp.where(eye, p_c[:, sls[p]], 0.0) + gmat[p][:, :pair]
            st1.append(_mm(mt, st0[p], _NN) + gmat[p][:, pair:])
        for p in pairs:
            rp_s = rts[p] + z[p][:, :pair]
            yv_s = z[p][:, pair:]
            rp = rp_s[:C] + rp_s[C:]
            yv = yv_s[:C] + yv_s[C:]
            y_ref[0, rows, sls[p]] = _mm(rp, st0[p], _NN) + yv
        return st1

    st = [st_ref[p] for p in pairs]
    for q in range(lw_ref.shape[1] // C):
        st = one_chunk(slice(q * C, (q + 1) * C), st)
    for p in pairs:
        st_ref[p] = st[p]


def _wkv_scan(r, lw, k, v, a, b, *, head):
    bsz, seq, n = r.shape
    rows = SCAN_CHUNK * SCAN_CHUNKS_PER_STEP
    spec = pl.BlockSpec((1, rows, n), lambda i, c: (i, c, 0))
    return pl.pallas_call(
        functools.partial(_scan_kernel, head=head, chunk=SCAN_CHUNK),
        grid=(bsz, seq // rows),
        in_specs=[spec] * 6,
        out_specs=spec,
        out_shape=jax.ShapeDtypeStruct((bsz, seq, n), F32),
        scratch_shapes=[pltpu.VMEM((n // (2 * head), 2 * head, 2 * head), F32)],
        compiler_params=_cparams(("arbitrary", "arbitrary"), 48),
        name="wkv_scan",
    )(r, lw, k, v, a, b)


def _post_kernel(y_ref, r_ref, k_ref, v_ref, g_ref, rk_ref, lnw_ref, lnb_ref, bd_ref, wpb_ref, gate_ref,
                 pa_ref, wout_ref, x_ref, n2_ref, x1_o, h2_o, *, head):
    bd = bd_ref[...]
    y = y_ref[...]
    inv = 1.0 / head
    mean = _head_sum(y, bd) * inv
    d = y - mean
    var = _head_sum(d * d, bd) * inv
    yn = d * lax.rsqrt(var + GN_EPS) * lnw_ref[...] + lnb_ref[...]
    r, k, v = (ref[...].astype(F32) for ref in (r_ref, k_ref, v_ref))
    bonus = _head_sum(r * k * rk_ref[...], bd) * v
    yb = ((yn + bonus) * g_ref[...].astype(F32)).astype(BF16)
    pb = jnp.dot(yb, wpb_ref[...], preferred_element_type=F32) * gate_ref[...].astype(F32)
    merged = (pa_ref[...].astype(F32) + pb).astype(BF16)
    x1 = x_ref[...] + jnp.dot(merged, wout_ref[...], preferred_element_type=F32)
    x1_o[...] = x1
    h2_o[...] = _rmsnorm(x1, n2_ref[...]).astype(h2_o.dtype)


def _rwkv_post(y, r, k, v, g, rk, lnw, lnb, bd, wpb, gates, pa, wout, x2, n2, *, head, tm):
    m, db = y.shape
    dm = x2.shape[1]
    row = lambda n, j=0: pl.BlockSpec((tm, n), lambda i: (i, j))
    vec = lambda n: pl.BlockSpec((1, n), lambda i: (0, 0))
    full = lambda a: pl.BlockSpec(a.shape, lambda i: (0, 0), pipeline_mode=pl.Buffered(1))
    return pl.pallas_call(
        functools.partial(_post_kernel, head=head),
        grid=(m // tm,),
        in_specs=[row(db), row(db), row(db), row(db), row(db), vec(db), vec(db), vec(db), full(bd), full(wpb),
                  row(dm, 1), row(dm), full(wout), row(dm), vec(dm)],
        out_specs=[row(dm), row(dm)],
        out_shape=[jax.ShapeDtypeStruct((m, dm), F32), jax.ShapeDtypeStruct((m, dm), BF16)],
        compiler_params=_cparams(("parallel",), 56),
        name="rwkv_post",
    )(y, r, k, v, g, rk, lnw, lnb, bd, wpb, gates, pa, wout, x2, n2)


def _ffn_kernel(h_ref, hp_ref, x1_ref, wg_ref, wv_ref, cwg_ref, cwv_ref, cbg_ref, cbv_ref, wd_ref, nf_ref,
                o_ref, hext_ref, ug_ref, uv_ref, acc_ref, *, seq, taps):
    tm = h_ref.shape[0]
    halo = hp_ref.shape[0]
    i = pl.program_id(0)
    j = pl.program_id(1)

    @pl.when(j == 0)
    def _():
        first = (i * tm) % seq == 0
        hext_ref[0:halo, :] = jnp.where(first, jnp.zeros_like(hp_ref), hp_ref[...])
        hext_ref[halo:, :] = h_ref[...]
        acc_ref[...] = jnp.zeros_like(acc_ref)

    hext = hext_ref[...]
    ug_ref[...] = jnp.dot(hext, wg_ref[...], preferred_element_type=F32)
    uv_ref[...] = jnp.dot(hext, wv_ref[...], preferred_element_type=F32)

    def conv(u_ref, cw_ref, cb_ref):
        out = cb_ref[...]
        for t in range(taps):
            off = halo - (taps - 1) + t
            out = out + cw_ref[t:t + 1, :] * u_ref[off:off + tm, :]
        return out

    act = (_gelu(conv(ug_ref, cwg_ref, cbg_ref)) * conv(uv_ref, cwv_ref, cbv_ref)).astype(BF16)
    acc_ref[...] += jnp.dot(act, wd_ref[...], preferred_element_type=F32)

    @pl.when(j == pl.num_programs(1) - 1)
    def _():
        o_ref[...] = _rmsnorm(x1_ref[...] + acc_ref[...], nf_ref[...])


def _conv_ffn(h2, x1, wup, cw, cb, wd, nf, *, seq, tm, tn):
    m, dm = h2.shape
    dff = wd.shape[0]
    nj = dff // tn
    taps = cw.shape[0]
    halo = SUBLANES_BF16
    blocks_per_tile = tm // halo
    return pl.pallas_call(
        functools.partial(_ffn_kernel, seq=seq, taps=taps),
        grid=(m // tm, nj),
        in_specs=[
            pl.BlockSpec((tm, dm), lambda i, j: (i, 0)),
            pl.BlockSpec((halo, dm), lambda i, j: (jnp.maximum(i * blocks_per_tile - 1, 0), 0)),
            pl.BlockSpec((tm, dm), lambda i, j: (i, 0)),
            pl.BlockSpec((dm, tn), lambda i, j: (0, j)),
            pl.BlockSpec((dm, tn), lambda i, j: (0, nj + j)),
            pl.BlockSpec((taps, tn), lambda i, j: (0, j)),
            pl.BlockSpec((taps, tn), lambda i, j: (0, nj + j)),
            pl.BlockSpec((1, tn), lambda i, j: (0, j)),
            pl.BlockSpec((1, tn), lambda i, j: (0, nj + j)),
            pl.BlockSpec((tn, dm), lambda i, j: (j, 0)),
            pl.BlockSpec((1, dm), lambda i, j: (0, 0)),
        ],
        out_specs=pl.BlockSpec((tm, dm), lambda i, j: (i, 0)),
        out_shape=jax.ShapeDtypeStruct((m, dm), F32),
        scratch_shapes=[
            pltpu.VMEM((tm + halo, dm), BF16),
            pltpu.VMEM((tm + halo, tn), F32),
            pltpu.VMEM((tm + halo, tn), F32),
            pltpu.VMEM((tm, dm), F32),
        ],
        compiler_params=_cparams(("parallel", "arbitrary"), 56),
        name="conv_ffn",
    )(h2, h2, x1, wup, wup, cw, cw, cb, cb, wd, nf)


def _pick_tile(n, want):
    t = min(want, n)
    while n % t:
        t //= 2
    return t


def _block_diag_ones(width, block):
    idx = jnp.arange(width) // block
    return (idx[:, None] == idx[None, :]).astype(BF16)


def _layer(x2, seq, p):
    m, dm = x2.shape
    da = p["gmlp_ln_w"].shape[0]
    db = p["rwkv_w0"].shape[0]
    n_heads, head = p["rwkv_rk"].shape
    lora_w = p["rwkv_w2"].shape[0]
    lora_a = p["rwkv_a2"].shape[0]
    lora_g = p["rwkv_g2"].shape[0]
    assert lora_w + lora_a == LANES and lora_g <= MXU_WIDTH
    d_b_in = 3 * db + lora_w + lora_a + lora_g
    w_in = p["w_in"]
    nf_pad = 3 * db + LANES + MXU_WIDTH
    tn_in = 2 * MXU_WIDTH
    nf_store = -(-nf_pad // tn_in) * tn_in
    w_all = jnp.concatenate([
        w_in[:, :2 * da].astype(BF16),
        jnp.pad(w_in[:, 2 * da:2 * da + d_b_in].astype(BF16), ((0, 0), (0, nf_store - d_b_in))),
        w_in[:, 2 * da + d_b_in:].astype(BF16)], axis=1)
    uv, feat, gates = _inproj(x2, p["norm1_g"].reshape(1, dm), w_all, widths=(2 * da, nf_store, 2 * dm),
                              tm=_pick_tile(m, 1024), tn=tn_in)

    pa = _gmlp(uv, gates, p["gmlp_ln_w"].reshape(1, da), p["gmlp_ln_b"].reshape(1, da), p["gmlp_ws"],
               p["gmlp_bs"].T, p["w_proj_a"].astype(BF16), tm=_pick_tile(seq, 512))

    mu = jnp.pad(p["mu_b"], (0, nf_pad - d_b_in)).reshape(1, nf_pad)
    zeros = jnp.zeros((lora_w, db), F32)
    w01 = jnp.concatenate([jnp.concatenate([p["rwkv_w2"], zeros], 1),
                           jnp.concatenate([jnp.zeros((lora_a, db), F32), p["rwkv_a2"]], 1)], 0).astype(BF16)
    g2p = jnp.pad(p["rwkv_g2"], ((0, MXU_WIDTH - lora_g), (0, 0))).astype(BF16)
    bd = _block_diag_ones(MXU_WIDTH, head)
    vecb = lambda a: a.reshape(1, db)
    r, lw, k, v, av, bv, g = _rwkv_prep(
        feat, mu, vecb(p["rwkv_w0"]), vecb(p["rwkv_a0"]), vecb(p["rwkv_kk"]), vecb(p["rwkv_ka"]), w01, g2p, bd,
        seq=seq, tm=_pick_tile(seq, 256), lora_wa=lora_w)
    bsz = m // seq
    to3 = lambda t: t.reshape(bsz, seq, db)
    y = _wkv_scan(to3(r), to3(lw), to3(k), to3(v), to3(av), to3(bv), head=head).reshape(m, db)
    x1, h2 = _rwkv_post(y, r, k, v, g, vecb(p["rwkv_rk"]), vecb(p["rwkv_ln_w"]), vecb(p["rwkv_ln_b"]), bd,
                        p["w_proj_b"].astype(BF16), gates, pa, p["w_out"].astype(BF16), x2,
                        p["norm2_g"].reshape(1, dm), head=head, tm=_pick_tile(seq, 256))
    return x1, h2


def kernel(x, norm1_g, w_in, mu_b, rwkv_w0, rwkv_w2, rwkv_a0, rwkv_a2, rwkv_g2, rwkv_kk, rwkv_ka, rwkv_rk, rwkv_ln_w, rwkv_ln_b, gmlp_ln_w, gmlp_ln_b, gmlp_ws, gmlp_bs, w_proj_a, w_proj_b, w_out, norm2_g, w_up, conv_w, conv_b, w_down, norm_f_g):
    bsz, seq, dm = x.shape
    depth = w_in.shape[0]
    assert depth == 1, "the fused FFN applies the final rmsnorm; one layer is supported"
    x2 = x.reshape(bsz * seq, dm)
    l = 0
    p = dict(norm1_g=norm1_g[l], w_in=w_in[l], mu_b=mu_b[l], rwkv_w0=rwkv_w0[l], rwkv_w2=rwkv_w2[l],
             rwkv_a0=rwkv_a0[l], rwkv_a2=rwkv_a2[l], rwkv_g2=rwkv_g2[l], rwkv_kk=rwkv_kk[l], rwkv_ka=rwkv_ka[l],
             rwkv_rk=rwkv_rk[l], rwkv_ln_w=rwkv_ln_w[l], rwkv_ln_b=rwkv_ln_b[l], gmlp_ln_w=gmlp_ln_w[l],
             gmlp_ln_b=gmlp_ln_b[l], gmlp_ws=gmlp_ws[l], gmlp_bs=gmlp_bs[l], w_proj_a=w_proj_a[l],
             w_proj_b=w_proj_b[l], w_out=w_out[l], norm2_g=norm2_g[l])
    x1, h2 = _layer(x2, seq, p)
    dff = w_down.shape[1]
    out = _conv_ffn(h2, x1, w_up[l].astype(BF16), conv_w[l], conv_b[l].reshape(1, -1), w_down[l].astype(BF16),
                    norm_f_g.reshape(1, dm), seq=seq, tm=_pick_tile(seq, 512), tn=_pick_tile(dff, 512))
    return out.reshape(bsz, seq, dm)
```

```python
import functools

import jax
import jax.numpy as jnp
from jax import lax
from jax.experimental import pallas as pl
from jax.experimental.pallas import tpu as pltpu

F32 = jnp.float32
BF16 = jnp.bfloat16

RMS_EPS = 1e-6
LN_EPS = 1e-5
GN_EPS = 64e-5
KK_NORM_FLOOR = 1e-12
EXP_NEG_HALF = 0.6065306597126334

LANES = 128
MXU_WIDTH = 256
SUBLANES_F32 = 8
SUBLANES_BF16 = 16

SCAN_CHUNK = 64
SCAN_CHUNKS_PER_STEP = 4


def _cparams(semantics, vmem_mb):
    return pltpu.CompilerParams(dimension_semantics=semantics, vmem_limit_bytes=vmem_mb * 1024 * 1024)


def _rmsnorm(x, g):
    ms = jnp.mean(x * x, axis=-1, keepdims=True)
    return x * lax.rsqrt(ms + RMS_EPS) * g


def _gelu(x):
    return 0.5 * x * (1.0 + lax.erf(x * (0.5 ** 0.5)))


def _sigmoid(x):
    return 0.5 * (jnp.tanh(0.5 * x) + 1.0)


def _bdot(a, b):
    return jnp.dot(a.astype(BF16), b.astype(BF16), preferred_element_type=F32)


def _inproj_kernel(x_ref, g_ref, w_ref, uv_ref, feat_ref, gate_ref, h_ref, *, n_uv, n_feat):
    j = pl.program_id(1)

    @pl.when(j == 0)
    def _():
        h_ref[...] = _rmsnorm(x_ref[...], g_ref[...]).astype(h_ref.dtype)

    def project():
        return jnp.dot(h_ref[...], w_ref[...], preferred_element_type=F32)

    @pl.when(j < n_uv)
    def _():
        uv_ref[...] = _gelu(project()).astype(uv_ref.dtype)

    @pl.when((j >= n_uv) & (j < n_uv + n_feat))
    def _():
        feat_ref[...] = project().astype(feat_ref.dtype)

    @pl.when(j >= n_uv + n_feat)
    def _():
        gate_ref[...] = _sigmoid(project()).astype(gate_ref.dtype)


def _inproj(x2, g, w, *, widths, tm, tn):
    m, d = x2.shape
    n_uv, n_feat, n_gate = (wd // tn for wd in widths)
    assert all(wd % tn == 0 for wd in widths) and sum(widths) == w.shape[1]
    return pl.pallas_call(
        functools.partial(_inproj_kernel, n_uv=n_uv, n_feat=n_feat),
        grid=(m // tm, n_uv + n_feat + n_gate),
        in_specs=[
            pl.BlockSpec((tm, d), lambda i, j: (i, 0)),
            pl.BlockSpec((1, d), lambda i, j: (0, 0)),
            pl.BlockSpec((d, tn), lambda i, j: (0, j)),
        ],
        out_specs=[
            pl.BlockSpec((tm, tn), lambda i, j: (i, jnp.minimum(j, n_uv - 1))),
            pl.BlockSpec((tm, tn), lambda i, j: (i, jnp.clip(j - n_uv, 0, n_feat - 1))),
            pl.BlockSpec((tm, tn), lambda i, j: (i, jnp.clip(j - n_uv - n_feat, 0, n_gate - 1))),
        ],
        out_shape=[
            jax.ShapeDtypeStruct((m, widths[0]), BF16),
            jax.ShapeDtypeStruct((m, widths[1]), BF16),
            jax.ShapeDtypeStruct((m, widths[2]), BF16),
        ],
        scratch_shapes=[pltpu.VMEM((tm, d), BF16)],
        compiler_params=_cparams(("parallel", "arbitrary"), 48),
        name="inproj",
    )(x2, g, w)


def _gmlp_kernel(u_ref, v_ref, lnw_ref, lnb_ref, ws_ref, bst_ref, wpa_ref, gate_ref, o_ref, ya_ref, *, chunk):
    tm, da = v_ref.shape
    n_groups = ws_ref.shape[0]
    gw = da // n_groups
    v = v_ref[...].astype(F32)
    mu = jnp.mean(v, axis=-1, keepdims=True)
    d = v - mu
    var = jnp.mean(d * d, axis=-1, keepdims=True)
    vn = (d * lax.rsqrt(var + LN_EPS) * lnw_ref[...] + lnb_ref[...]).astype(BF16)
    row = lax.broadcasted_iota(jnp.int32, (chunk, chunk), 0)
    col = lax.broadcasted_iota(jnp.int32, (chunk, chunk), 1)
    causal = row >= col
    for g in range(n_groups):
        wm = jnp.where(causal, ws_ref[g], 0.0).astype(BF16)
        bias = bst_ref[:, g:g + 1]
        for c in range(tm // chunk):
            rs = slice(c * chunk, (c + 1) * chunk)
            cs = slice(g * gw, (g + 1) * gw)
            mixed = jnp.dot(wm, vn[rs, cs], preferred_element_type=F32) + bias
            ya_ref[rs, cs] = (u_ref[rs, cs].astype(F32) * mixed).astype(BF16)
    pa = jnp.dot(ya_ref[...], wpa_ref[...], preferred_element_type=F32)
    o_ref[...] = (pa * gate_ref[...].astype(F32)).astype(o_ref.dtype)


def _gmlp(uv, gates, lnw, lnb, ws, bst, wpa, *, tm):
    m = uv.shape[0]
    da = uv.shape[1] // 2
    dm = wpa.shape[1]
    chunk = ws.shape[1]
    return pl.pallas_call(
        functools.partial(_gmlp_kernel, chunk=chunk),
        grid=(m // tm,),
        in_specs=[
            pl.BlockSpec((tm, da), lambda i: (i, 0)),
            pl.BlockSpec((tm, da), lambda i: (i, 1)),
            pl.BlockSpec((1, da), lambda i: (0, 0)),
            pl.BlockSpec((1, da), lambda i: (0, 0)),
            pl.BlockSpec(ws.shape, lambda i: (0, 0, 0)),
            pl.BlockSpec(bst.shape, lambda i: (0, 0)),
            pl.BlockSpec(wpa.shape, lambda i: (0, 0)),
            pl.BlockSpec((tm, dm), lambda i: (i, 0)),
        ],
        out_specs=pl.BlockSpec((tm, dm), lambda i: (i, 0)),
        out_shape=jax.ShapeDtypeStruct((m, dm), BF16),
        scratch_shapes=[pltpu.VMEM((tm, da), BF16)],
        compiler_params=_cparams(("parallel",), 48),
        name="gmlp",
    )(uv, uv, lnw, lnb, ws, bst, wpa, gates)


def _head_sum(x, bd):
    xb = x.astype(BF16)
    outs = [jnp.dot(xb[:, j:j + MXU_WIDTH], bd, preferred_element_type=F32) for j in range(0, x.shape[1], MXU_WIDTH)]
    return jnp.concatenate(outs, axis=1)


def _prep_kernel(feat_ref, prev_ref, mu_ref, w0_ref, a0_ref, kk_ref, ka_ref, w01_ref, g2_ref, bd_ref,
                 r_o, lw_o, k_o, v_o, a_o, b_o, g_o, *, seq, db, lora_wa):
    tm = feat_ref.shape[0]
    i = pl.program_id(0)
    feat = feat_ref[...].astype(F32)
    first = (i * tm) % seq == 0
    halo = prev_ref.shape[0]
    prev = jnp.where(first, 0.0, prev_ref[halo - 1:halo, :].astype(F32))
    rows = lax.broadcasted_iota(jnp.int32, (tm, 1), 0)
    shifted = jnp.where(rows == 0, prev, pltpu.roll(feat, 1, 0))
    fm = feat + (shifted - feat) * mu_ref[...]
    r = fm[:, 0:db]
    k = fm[:, db:2 * db]
    v = fm[:, 2 * db:3 * db]
    l0 = fm[:, 3 * db:3 * db + LANES]
    l1 = fm[:, 3 * db + LANES:]
    lane = lax.broadcasted_iota(jnp.int32, l0.shape, 1)
    x0 = jnp.where(lane < lora_wa, jnp.tanh(l0), l0)
    wa = _bdot(x0, w01_ref[...])
    g = _bdot(_sigmoid(l1), g2_ref[...])
    w_raw = w0_ref[...] + wa[:, :db]
    a = _sigmoid(a0_ref[...] + wa[:, db:])
    lw = -EXP_NEG_HALF * _sigmoid(w_raw)
    kkv = k * kk_ref[...]
    kkn = kkv * lax.rsqrt(jnp.maximum(_head_sum(kkv * kkv, bd_ref[...]), KK_NORM_FLOOR ** 2))
    kp = k * (1.0 + (a - 1.0) * ka_ref[...])
    lw_o[...] = lw
    for o_ref, val in ((r_o, r), (k_o, kp), (v_o, v), (a_o, -kkn), (b_o, kkn * a), (g_o, g)):
        o_ref[...] = val.astype(o_ref.dtype)


def _rwkv_prep(feat, mu, w0, a0, kk, ka, w01, g2p, bd, *, seq, tm, lora_wa):
    m = feat.shape[0]
    nf = mu.shape[1]
    db = w0.shape[1]
    vec = lambda n: pl.BlockSpec((1, n), lambda i: (0, 0))
    full = lambda a: pl.BlockSpec(a.shape, lambda i: (0, 0))
    out_spec = pl.BlockSpec((tm, db), lambda i: (i, 0))
    halo = SUBLANES_BF16 if feat.dtype == BF16 else SUBLANES_F32
    blocks_per_tile = tm // halo
    return pl.pallas_call(
        functools.partial(_prep_kernel, seq=seq, db=db, lora_wa=lora_wa),
        grid=(m // tm,),
        in_specs=[
            pl.BlockSpec((tm, nf), lambda i: (i, 0)),
            pl.BlockSpec((halo, nf), lambda i: (jnp.maximum(i * blocks_per_tile - 1, 0), 0)),
            vec(nf), vec(db), vec(db), vec(db), vec(db), full(w01), full(g2p), full(bd),
        ],
        out_specs=[out_spec] * 7,
        out_shape=[jax.ShapeDtypeStruct((m, db), F32 if i == 1 else BF16) for i in range(7)],
        compiler_params=_cparams(("parallel",), 48),
        name="rwkv_prep",
    )(feat, feat, mu, w0, a0, kk, ka, w01, g2p, bd)


def _mm(a, b, dims):
    return lax.dot_general(a.astype(BF16), b.astype(BF16), (dims, ((), ())), preferred_element_type=F32)


_NN = ((1,), (0,))
_NT = ((1,), (1,))
_TN = ((0,), (0,))


def _scan_kernel(r_ref, lw_ref, k_ref, v_ref, a_ref, b_ref, y_ref, st_ref, *, head, chunk):
    c = pl.program_id(1)
    C = chunk
    n = lw_ref.shape[2]
    pair = 2 * head
    assert pair == LANES and 2 * C == LANES
    pairs = range(n // pair)
    sls = [slice(p * pair, (p + 1) * pair) for p in pairs]

    @pl.when(c == 0)
    def _():
        st_ref[...] = jnp.zeros_like(st_ref)

    rc = lax.broadcasted_iota(jnp.int32, (C, C), 0)
    cc = lax.broadcasted_iota(jnp.int32, (C, C), 1)
    tri = (rc >= cc).astype(F32)
    lane = lax.broadcasted_iota(jnp.int32, (C, pair), 1)
    m0 = lane < head
    ri = lax.broadcasted_iota(jnp.int32, (pair, pair), 0)
    ci = lax.broadcasted_iota(jnp.int32, (pair, pair), 1)
    same = (ri >= C) == (ci >= C)
    strict = same & (ri > ci)
    incl = same & (ri >= ci)
    eye = ri == ci
    eye_f = eye.astype(F32)
    zero = jnp.zeros((pair, pair), F32)

    def stack(x):
        return jnp.concatenate([jnp.where(m0, x, 0.0), jnp.where(m0, 0.0, x)], axis=0)

    def one_chunk(rows, st0):
        lw = lw_ref[0, rows, :]
        L = lax.dot_general(tri, lw, (_NN, ((), ())), precision=lax.Precision.HIGHEST, preferred_element_type=F32)
        LC = L[C - 1:C, :]
        p_in = jnp.exp(L)
        p_ex = jnp.exp(L - lw)
        p_inv = jnp.exp(-L)
        p_end = jnp.exp(LC - L)
        p_c = jnp.exp(LC)
        r, k, v, a, b = (ref[0, rows, :].astype(F32) for ref in (r_ref, k_ref, v_ref, a_ref, b_ref))
        rt = r * p_in
        at = a * p_ex
        bt = b * p_inv
        kt = k * p_inv
        bh = b * p_end
        kh = k * p_end
        ats, rts, bts, kts, vs, bhs, khs = ([stack(x[:, sl]) for sl in sls] for x in (at, rt, bt, kt, v, bh, kh))
        amat = [_mm(jnp.concatenate([ats[p], rts[p]], 0), jnp.concatenate([bts[p], kts[p]], 0), _NT) for p in pairs]
        aab = [jnp.where(strict, amat[p][:pair, :pair], 0.0) for p in pairs]
        aak = [jnp.where(strict, amat[p][:pair, pair:], 0.0) for p in pairs]
        arb = [jnp.where(incl, amat[p][pair:, :pair], 0.0) for p in pairs]
        ark = [jnp.where(incl, amat[p][pair:, pair:], 0.0) for p in pairs]
        t_inv = [eye_f + aab[p] for p in pairs]
        aj = [_mm(aab[p], aab[p], _NN) for p in pairs]
        w1 = [_mm(aak[p], vs[p], _NN) for p in pairs]
        n_factors = C.bit_length() - 1
        for _ in range(n_factors - 2):
            res = [_mm(aj[p], jnp.concatenate([t_inv[p], aj[p]], 1), _NN) for p in pairs]
            t_inv = [t_inv[p] + res[p][:, :pair] for p in pairs]
            aj = [res[p][:, pair:] for p in pairs]
        t_inv = [t_inv[p] + _mm(aj[p], t_inv[p], _NN) for p in pairs]
        x = [_mm(t_inv[p], jnp.concatenate([ats[p], w1[p]], 1), _NN) for p in pairs]
        xv = [jnp.concatenate([x[p], jnp.concatenate([zero, vs[p]], 1)], 0) for p in pairs]
        z = [_mm(jnp.concatenate([arb[p], ark[p]], 1), xv[p], _NN) for p in pairs]
        gmat = [_mm(jnp.concatenate([bhs[p], khs[p]], 0), xv[p], _TN) for p in pairs]
        st1 = []
        for p in pairs:
            mt = jnp.where(eye, p_c[:, sls[p]], 0.0) + gmat[p][:, :pair]
            st1.append(_mm(mt, st0[p], _NN) + gmat[p][:, pair:])
        for p in pairs:
            rp_s = rts[p] + z[p][:, :pair]
            yv_s = z[p][:, pair:]
            rp = rp_s[:C] + rp_s[C:]
            yv = yv_s[:C] + yv_s[C:]
            y_ref[0, rows, sls[p]] = _mm(rp, st0[p], _NN) + yv
        return st1

    st = [st_ref[p] for p in pairs]
    for q in range(lw_ref.shape[1] // C):
        st = one_chunk(slice(q * C, (q + 1) * C), st)
    for p in pairs:
        st_ref[p] = st[p]


def _wkv_scan(r, lw, k, v, a, b, *, head):
    bsz, seq, n = r.shape
    rows = SCAN_CHUNK * SCAN_CHUNKS_PER_STEP
    spec = pl.BlockSpec((1, rows, n), lambda i, c: (i, c, 0))
    return pl.pallas_call(
        functools.partial(_scan_kernel, head=head, chunk=SCAN_CHUNK),
        grid=(bsz, seq // rows),
        in_specs=[spec] * 6,
        out_specs=spec,
        out_shape=jax.ShapeDtypeStruct((bsz, seq, n), F32),
        scratch_shapes=[pltpu.VMEM((n // (2 * head), 2 * head, 2 * head), F32)],
        compiler_params=_cparams(("arbitrary", "arbitrary"), 48),
        name="wkv_scan",
    )(r, lw, k, v, a, b)


def _post_kernel(y_ref, r_ref, k_ref, v_ref, g_ref, rk_ref, lnw_ref, lnb_ref, bd_ref, wpb_ref, gate_ref,
                 pa_ref, wout_ref, x_ref, n2_ref, x1_o, h2_o, *, head):
    bd = bd_ref[...]
    y = y_ref[...]
    inv = 1.0 / head
    mean = _head_sum(y, bd) * inv
    d = y - mean
    var = _head_sum(d * d, bd) * inv
    yn = d * lax.rsqrt(var + GN_EPS) * lnw_ref[...] + lnb_ref[...]
    r, k, v = (ref[...].astype(F32) for ref in (r_ref, k_ref, v_ref))
    bonus = _head_sum(r * k * rk_ref[...], bd) * v
    yb = ((yn + bonus) * g_ref[...].astype(F32)).astype(BF16)
    pb = jnp.dot(yb, wpb_ref[...], preferred_element_type=F32) * gate_ref[...].astype(F32)
    merged = (pa_ref[...].astype(F32) + pb).astype(BF16)
    x1 = x_ref[...] + jnp.dot(merged, wout_ref[...], preferred_element_type=F32)
    x1_o[...] = x1
    h2_o[...] = _rmsnorm(x1, n2_ref[...]).astype(h2_o.dtype)


def _rwkv_post(y, r, k, v, g, rk, lnw, lnb, bd, wpb, gates, pa, wout, x2, n2, *, head, tm):
    m, db = y.shape
    dm = x2.shape[1]
    row = lambda n, j=0: pl.BlockSpec((tm, n), lambda i: (i, j))
    vec = lambda n: pl.BlockSpec((1, n), lambda i: (0, 0))
    full = lambda a: pl.BlockSpec(a.shape, lambda i: (0, 0), pipeline_mode=pl.Buffered(1))
    return pl.pallas_call(
        functools.partial(_post_kernel, head=head),
        grid=(m // tm,),
        in_specs=[row(db), row(db), row(db), row(db), row(db), vec(db), vec(db), vec(db), full(bd), full(wpb),
                  row(dm, 1), row(dm), full(wout), row(dm), vec(dm)],
        out_specs=[row(dm), row(dm)],
        out_shape=[jax.ShapeDtypeStruct((m, dm), F32), jax.ShapeDtypeStruct((m, dm), BF16)],
        compiler_params=_cparams(("parallel",), 56),
        name="rwkv_post",
    )(y, r, k, v, g, rk, lnw, lnb, bd, wpb, gates, pa, wout, x2, n2)


def _ffn_kernel(h_ref, hp_ref, x1_ref, wup_ref, cw_ref, cb_ref, wd_ref, nf_ref,
                o_ref, hext_ref, u_ref, *, seq, taps):
    tm = h_ref.shape[0]
    halo = hp_ref.shape[0]
    tn = wd_ref.shape[0]
    i = pl.program_id(0)
    j = pl.program_id(1)

    @pl.when(j == 0)
    def _():
        first = (i * tm) % seq == 0
        hext_ref[0:halo, :] = jnp.where(first, jnp.zeros_like(hp_ref), hp_ref[...])
        hext_ref[halo:, :] = h_ref[...]
        o_ref[...] = x1_ref[...]

    u_ref[...] = jnp.dot(hext_ref[...], wup_ref[...], preferred_element_type=F32)

    def conv(cs):
        out = cb_ref[:, cs]
        for t in range(taps):
            off = halo - (taps - 1) + t
            out = out + cw_ref[t:t + 1, cs] * u_ref[off:off + tm, cs]
        return out

    act = (_gelu(conv(slice(0, tn))) * conv(slice(tn, 2 * tn))).astype(BF16)
    o_ref[...] += jnp.dot(act, wd_ref[...], preferred_element_type=F32)

    @pl.when(j == pl.num_programs(1) - 1)
    def _():
        o_ref[...] = _rmsnorm(o_ref[...], nf_ref[...])


def _pair_tiles(w, tn):
    lead = w.shape[:-1]
    nj = w.shape[-1] // (2 * tn)
    return w.reshape(*lead, 2, nj, tn).swapaxes(-3, -2).reshape(*lead, 2 * nj * tn)


def _conv_ffn(h2, x1, wup, cw, cb, wd, nf, *, seq, tm, tn):
    m, dm = h2.shape
    dff = wd.shape[0]
    nj = dff // tn
    taps = cw.shape[0]
    halo = SUBLANES_BF16
    blocks_per_tile = tm // halo
    wup, cw, cb = (_pair_tiles(t, tn) for t in (wup, cw, cb))
    return pl.pallas_call(
        functools.partial(_ffn_kernel, seq=seq, taps=taps),
        grid=(m // tm, nj),
        in_specs=[
            pl.BlockSpec((tm, dm), lambda i, j: (i, 0), pipeline_mode=pl.Buffered(1)),
            pl.BlockSpec((halo, dm), lambda i, j: (jnp.maximum(i * blocks_per_tile - 1, 0), 0)),
            pl.BlockSpec((tm, dm), lambda i, j: (i, 0)),
            pl.BlockSpec((dm, 2 * tn), lambda i, j: (0, j)),
            pl.BlockSpec((taps, 2 * tn), lambda i, j: (0, j)),
            pl.BlockSpec((1, 2 * tn), lambda i, j: (0, j)),
            pl.BlockSpec((tn, dm), lambda i, j: (j, 0)),
            pl.BlockSpec((1, dm), lambda i, j: (0, 0)),
        ],
        out_specs=pl.BlockSpec((tm, dm), lambda i, j: (i, 0)),
        out_shape=jax.ShapeDtypeStruct((m, dm), F32),
        scratch_shapes=[
            pltpu.VMEM((tm + halo, dm), BF16),
            pltpu.VMEM((tm + halo, 2 * tn), F32),
        ],
        compiler_params=_cparams(("parallel", "arbitrary"), 62),
        name="conv_ffn",
    )(h2, h2, x1, wup, cw, cb, wd, nf)


def _pick_tile(n, want):
    t = min(want, n)
    while n % t:
        t //= 2
    return t


def _block_diag_ones(width, block):
    idx = jnp.arange(width) // block
    return (idx[:, None] == idx[None, :]).astype(BF16)


def _layer(x2, seq, p):
    m, dm = x2.shape
    da = p["gmlp_ln_w"].shape[0]
    db = p["rwkv_w0"].shape[0]
    n_heads, head = p["rwkv_rk"].shape
    lora_w = p["rwkv_w2"].shape[0]
    lora_a = p["rwkv_a2"].shape[0]
    lora_g = p["rwkv_g2"].shape[0]
    assert lora_w + lora_a == LANES and lora_g <= MXU_WIDTH
    d_b_in = 3 * db + lora_w + lora_a + lora_g
    w_in = p["w_in"]
    nf_pad = 3 * db + LANES + MXU_WIDTH
    tn_in = 2 * MXU_WIDTH
    nf_store = -(-nf_pad // tn_in) * tn_in
    w_all = jnp.concatenate([
        w_in[:, :2 * da].astype(BF16),
        jnp.pad(w_in[:, 2 * da:2 * da + d_b_in].astype(BF16), ((0, 0), (0, nf_store - d_b_in))),
        w_in[:, 2 * da + d_b_in:].astype(BF16)], axis=1)
    uv, feat, gates = _inproj(x2, p["norm1_g"].reshape(1, dm), w_all, widths=(2 * da, nf_store, 2 * dm),
                              tm=_pick_tile(m, 1024), tn=tn_in)

    pa = _gmlp(uv, gates, p["gmlp_ln_w"].reshape(1, da), p["gmlp_ln_b"].reshape(1, da), p["gmlp_ws"],
               p["gmlp_bs"].T, p["w_proj_a"].astype(BF16), tm=_pick_tile(seq, 512))

    mu = jnp.pad(p["mu_b"], (0, nf_pad - d_b_in)).reshape(1, nf_pad)
    zeros = jnp.zeros((lora_w, db), F32)
    w01 = jnp.concatenate([jnp.concatenate([p["rwkv_w2"], zeros], 1),
                           jnp.concatenate([jnp.zeros((lora_a, db), F32), p["rwkv_a2"]], 1)], 0).astype(BF16)
    g2p = jnp.pad(p["rwkv_g2"], ((0, MXU_WIDTH - lora_g), (0, 0))).astype(BF16)
    bd = _block_diag_ones(MXU_WIDTH, head)
    vecb = lambda a: a.reshape(1, db)
    r, lw, k, v, av, bv, g = _rwkv_prep(
        feat, mu, vecb(p["rwkv_w0"]), vecb(p["rwkv_a0"]), vecb(p["rwkv_kk"]), vecb(p["rwkv_ka"]), w01, g2p, bd,
        seq=seq, tm=_pick_tile(seq, 256), lora_wa=lora_w)
    bsz = m // seq
    to3 = lambda t: t.reshape(bsz, seq, db)
    y = _wkv_scan(to3(r), to3(lw), to3(k), to3(v), to3(av), to3(bv), head=head).reshape(m, db)
    x1, h2 = _rwkv_post(y, r, k, v, g, vecb(p["rwkv_rk"]), vecb(p["rwkv_ln_w"]), vecb(p["rwkv_ln_b"]), bd,
                        p["w_proj_b"].astype(BF16), gates, pa, p["w_out"].astype(BF16), x2,
                        p["norm2_g"].reshape(1, dm), head=head, tm=_pick_tile(seq, 256))
    return x1, h2


def kernel(x, norm1_g, w_in, mu_b, rwkv_w0, rwkv_w2, rwkv_a0, rwkv_a2, rwkv_g2, rwkv_kk, rwkv_ka, rwkv_rk, rwkv_ln_w, rwkv_ln_b, gmlp_ln_w, gmlp_ln_b, gmlp_ws, gmlp_bs, w_proj_a, w_proj_b, w_out, norm2_g, w_up, conv_w, conv_b, w_down, norm_f_g):
    bsz, seq, dm = x.shape
    depth = w_in.shape[0]
    assert depth == 1, "the fused FFN applies the final rmsnorm; one layer is supported"
    x2 = x.reshape(bsz * seq, dm)
    l = 0
    p = dict(norm1_g=norm1_g[l], w_in=w_in[l], mu_b=mu_b[l], rwkv_w0=rwkv_w0[l], rwkv_w2=rwkv_w2[l],
             rwkv_a0=rwkv_a0[l], rwkv_a2=rwkv_a2[l], rwkv_g2=rwkv_g2[l], rwkv_kk=rwkv_kk[l], rwkv_ka=rwkv_ka[l],
             rwkv_rk=rwkv_rk[l], rwkv_ln_w=rwkv_ln_w[l], rwkv_ln_b=rwkv_ln_b[l], gmlp_ln_w=gmlp_ln_w[l],
             gmlp_ln_b=gmlp_ln_b[l], gmlp_ws=gmlp_ws[l], gmlp_bs=gmlp_bs[l], w_proj_a=w_proj_a[l],
             w_proj_b=w_proj_b[l], w_out=w_out[l], norm2_g=norm2_g[l])
    x1, h2 = _layer(x2, seq, p)
    dff = w_down.shape[1]
    out = _conv_ffn(h2, x1, w_up[l].astype(BF16), conv_w[l], conv_b[l].reshape(1, -1), w_down[l].astype(BF16),
                    norm_f_g.reshape(1, dm), seq=seq, tm=_pick_tile(seq, 1024), tn=_pick_tile(dff, 512))
    return out.reshape(bsz, seq, dm)
```

```python
import functools

import jax
import jax.numpy as jnp
from jax import lax
from jax.experimental import pallas as pl
from jax.experimental.pallas import tpu as pltpu

F32 = jnp.float32
BF16 = jnp.bfloat16

RMS_EPS = 1e-6
LN_EPS = 1e-5
GN_EPS = 64e-5
KK_NORM_FLOOR = 1e-12
EXP_NEG_HALF = 0.6065306597126334

LANES = 128
MXU_WIDTH = 256
SUBLANES_F32 = 8
SUBLANES_BF16 = 16

SCAN_CHUNK = 64
SCAN_CHUNKS_PER_STEP = 8


def _cparams(semantics, vmem_mb):
    return pltpu.CompilerParams(dimension_semantics=semantics, vmem_limit_bytes=vmem_mb * 1024 * 1024)


def _rmsnorm(x, g):
    ms = jnp.mean(x * x, axis=-1, keepdims=True)
    return x * lax.rsqrt(ms + RMS_EPS) * g


def _gelu(x):
    return 0.5 * x * (1.0 + lax.erf(x * (0.5 ** 0.5)))


def _sigmoid(x):
    return 0.5 * (jnp.tanh(0.5 * x) + 1.0)


def _bdot(a, b):
    return jnp.dot(a.astype(BF16), b.astype(BF16), preferred_element_type=F32)


def _inproj_kernel(x_ref, g_ref, w_ref, uv_ref, feat_ref, gate_ref, h_ref, *, n_uv, n_feat):
    j = pl.program_id(1)

    @pl.when(j == 0)
    def _():
        h_ref[...] = _rmsnorm(x_ref[...], g_ref[...]).astype(h_ref.dtype)

    def project():
        return jnp.dot(h_ref[...], w_ref[...], preferred_element_type=F32)

    @pl.when(j < n_uv)
    def _():
        uv_ref[...] = _gelu(project()).astype(uv_ref.dtype)

    @pl.when((j >= n_uv) & (j < n_uv + n_feat))
    def _():
        feat_ref[...] = project().astype(feat_ref.dtype)

    @pl.when(j >= n_uv + n_feat)
    def _():
        gate_ref[...] = _sigmoid(project()).astype(gate_ref.dtype)


def _inproj(x2, g, w, *, widths, tm, tn):
    m, d = x2.shape
    n_uv, n_feat, n_gate = (wd // tn for wd in widths)
    assert all(wd % tn == 0 for wd in widths) and sum(widths) == w.shape[1]
    return pl.pallas_call(
        functools.partial(_inproj_kernel, n_uv=n_uv, n_feat=n_feat),
        grid=(m // tm, n_uv + n_feat + n_gate),
        in_specs=[
            pl.BlockSpec((tm, d), lambda i, j: (i, 0)),
            pl.BlockSpec((1, d), lambda i, j: (0, 0)),
            pl.BlockSpec((d, tn), lambda i, j: (0, j)),
        ],
        out_specs=[
            pl.BlockSpec((tm, tn), lambda i, j: (i, jnp.minimum(j, n_uv - 1))),
            pl.BlockSpec((tm, tn), lambda i, j: (i, jnp.clip(j - n_uv, 0, n_feat - 1))),
            pl.BlockSpec((tm, tn), lambda i, j: (i, jnp.clip(j - n_uv - n_feat, 0, n_gate - 1))),
        ],
        out_shape=[
            jax.ShapeDtypeStruct((m, widths[0]), BF16),
            jax.ShapeDtypeStruct((m, widths[1]), BF16),
            jax.ShapeDtypeStruct((m, widths[2]), BF16),
        ],
        scratch_shapes=[pltpu.VMEM((tm, d), BF16)],
        compiler_params=_cparams(("parallel", "arbitrary"), 48),
        name="inproj",
    )(x2, g, w)


def _gmlp_kernel(u_ref, v_ref, lnw_ref, lnb_ref, ws_ref, bst_ref, wpa_ref, gate_ref, o_ref, ya_ref, *, chunk):
    tm, da = v_ref.shape
    n_groups = ws_ref.shape[0]
    gw = da // n_groups
    v = v_ref[...].astype(F32)
    mu = jnp.mean(v, axis=-1, keepdims=True)
    d = v - mu
    var = jnp.mean(d * d, axis=-1, keepdims=True)
    vn = (d * lax.rsqrt(var + LN_EPS) * lnw_ref[...] + lnb_ref[...]).astype(BF16)
    row = lax.broadcasted_iota(jnp.int32, (chunk, chunk), 0)
    col = lax.broadcasted_iota(jnp.int32, (chunk, chunk), 1)
    causal = row >= col
    for g in range(n_groups):
        wm = jnp.where(causal, ws_ref[g], 0.0).astype(BF16)
        bias = bst_ref[:, g:g + 1]
        for c in range(tm // chunk):
            rs = slice(c * chunk, (c + 1) * chunk)
            cs = slice(g * gw, (g + 1) * gw)
            mixed = jnp.dot(wm, vn[rs, cs], preferred_element_type=F32) + bias
            ya_ref[rs, cs] = (u_ref[rs, cs].astype(F32) * mixed).astype(BF16)
    pa = jnp.dot(ya_ref[...], wpa_ref[...], preferred_element_type=F32)
    o_ref[...] = (pa * gate_ref[...].astype(F32)).astype(o_ref.dtype)


def _gmlp(uv, gates, lnw, lnb, ws, bst, wpa, *, tm):
    m = uv.shape[0]
    da = uv.shape[1] // 2
    dm = wpa.shape[1]
    chunk = ws.shape[1]
    return pl.pallas_call(
        functools.partial(_gmlp_kernel, chunk=chunk),
        grid=(m // tm,),
        in_specs=[
            pl.BlockSpec((tm, da), lambda i: (i, 0)),
            pl.BlockSpec((tm, da), lambda i: (i, 1)),
            pl.BlockSpec((1, da), lambda i: (0, 0)),
            pl.BlockSpec((1, da), lambda i: (0, 0)),
            pl.BlockSpec(ws.shape, lambda i: (0, 0, 0)),
            pl.BlockSpec(bst.shape, lambda i: (0, 0)),
            pl.BlockSpec(wpa.shape, lambda i: (0, 0)),
            pl.BlockSpec((tm, dm), lambda i: (i, 0)),
        ],
        out_specs=pl.BlockSpec((tm, dm), lambda i: (i, 0)),
        out_shape=jax.ShapeDtypeStruct((m, dm), BF16),
        scratch_shapes=[pltpu.VMEM((tm, da), BF16)],
        compiler_params=_cparams(("parallel",), 48),
        name="gmlp",
    )(uv, uv, lnw, lnb, ws, bst, wpa, gates)


def _head_sum(x, bd):
    xb = x.astype(BF16)
    outs = [jnp.dot(xb[:, j:j + MXU_WIDTH], bd, preferred_element_type=F32) for j in range(0, x.shape[1], MXU_WIDTH)]
    return jnp.concatenate(outs, axis=1)


def _prep_kernel(feat_ref, prev_ref, mu_ref, w0_ref, a0_ref, kk_ref, ka_ref, w01_ref, g2_ref, bd_ref,
                 r_o, lw_o, k_o, v_o, a_o, b_o, g_o, *, seq, db, lora_wa):
    tm = feat_ref.shape[0]
    i = pl.program_id(0)
    feat = feat_ref[...].astype(F32)
    first = (i * tm) % seq == 0
    halo = prev_ref.shape[0]
    prev = jnp.where(first, 0.0, prev_ref[halo - 1:halo, :].astype(F32))
    rows = lax.broadcasted_iota(jnp.int32, (tm, 1), 0)
    shifted = jnp.where(rows == 0, prev, pltpu.roll(feat, 1, 0))
    fm = feat + (shifted - feat) * mu_ref[...]
    r = fm[:, 0:db]
    k = fm[:, db:2 * db]
    v = fm[:, 2 * db:3 * db]
    l0 = fm[:, 3 * db:3 * db + LANES]
    l1 = fm[:, 3 * db + LANES:]
    lane = lax.broadcasted_iota(jnp.int32, l0.shape, 1)
    x0 = jnp.where(lane < lora_wa, jnp.tanh(l0), l0)
    wa = _bdot(x0, w01_ref[...])
    g = _bdot(_sigmoid(l1), g2_ref[...])
    w_raw = w0_ref[...] + wa[:, :db]
    a = _sigmoid(a0_ref[...] + wa[:, db:])
    lw = -EXP_NEG_HALF * _sigmoid(w_raw)
    kkv = k * kk_ref[...]
    kkn = kkv * lax.rsqrt(jnp.maximum(_head_sum(kkv * kkv, bd_ref[...]), KK_NORM_FLOOR ** 2))
    kp = k * (1.0 + (a - 1.0) * ka_ref[...])
    lw_o[...] = lw
    for o_ref, val in ((r_o, r), (k_o, kp), (v_o, v), (a_o, -kkn), (b_o, kkn * a), (g_o, g)):
        o_ref[...] = val.astype(o_ref.dtype)


def _rwkv_prep(feat, mu, w0, a0, kk, ka, w01, g2p, bd, *, seq, tm, lora_wa):
    m = feat.shape[0]
    nf = mu.shape[1]
    db = w0.shape[1]
    vec = lambda n: pl.BlockSpec((1, n), lambda i: (0, 0))
    full = lambda a: pl.BlockSpec(a.shape, lambda i: (0, 0))
    out_spec = pl.BlockSpec((tm, db), lambda i: (i, 0))
    halo = SUBLANES_BF16 if feat.dtype == BF16 else SUBLANES_F32
    blocks_per_tile = tm // halo
    return pl.pallas_call(
        functools.partial(_prep_kernel, seq=seq, db=db, lora_wa=lora_wa),
        grid=(m // tm,),
        in_specs=[
            pl.BlockSpec((tm, nf), lambda i: (i, 0)),
            pl.BlockSpec((halo, nf), lambda i: (jnp.maximum(i * blocks_per_tile - 1, 0), 0)),
            vec(nf), vec(db), vec(db), vec(db), vec(db), full(w01), full(g2p), full(bd),
        ],
        out_specs=[out_spec] * 7,
        out_shape=[jax.ShapeDtypeStruct((m, db), F32 if i == 1 else BF16) for i in range(7)],
        compiler_params=_cparams(("parallel",), 48),
        name="rwkv_prep",
    )(feat, feat, mu, w0, a0, kk, ka, w01, g2p, bd)


def _mm(a, b, dims):
    return lax.dot_general(a.astype(BF16), b.astype(BF16), (dims, ((), ())), preferred_element_type=F32)


_NN = ((1,), (0,))
_NT = ((1,), (1,))
_TN = ((0,), (0,))


def _scan_kernel(r_ref, lw_ref, k_ref, v_ref, a_ref, b_ref, y_ref, st_ref, *, head, chunk):
    c = pl.program_id(1)
    C = chunk
    n = lw_ref.shape[2]
    pair = 2 * head
    assert pair == LANES and 2 * C == LANES
    pairs = range(n // pair)
    sls = [slice(p * pair, (p + 1) * pair) for p in pairs]

    @pl.when(c == 0)
    def _():
        st_ref[...] = jnp.zeros_like(st_ref)

    rc = lax.broadcasted_iota(jnp.int32, (C, C), 0)
    cc = lax.broadcasted_iota(jnp.int32, (C, C), 1)
    tri = (rc >= cc).astype(F32)
    lane = lax.broadcasted_iota(jnp.int32, (C, pair), 1)
    m0 = lane < head
    ri = lax.broadcasted_iota(jnp.int32, (pair, pair), 0)
    ci = lax.broadcasted_iota(jnp.int32, (pair, pair), 1)
    same = (ri >= C) == (ci >= C)
    strict = same & (ri > ci)
    incl = same & (ri >= ci)
    eye = ri == ci
    eye_f = eye.astype(F32)
    zero = jnp.zeros((pair, pair), F32)

    def stack(x):
        return jnp.concatenate([jnp.where(m0, x, 0.0), jnp.where(m0, 0.0, x)], axis=0)

    def one_chunk(rows, st0):
        lw = lw_ref[0, rows, :]
        L = lax.dot_general(tri, lw, (_NN, ((), ())), precision=lax.Precision.HIGHEST, preferred_element_type=F32)
        LC = L[C - 1:C, :]
        p_in = jnp.exp(L)
        p_ex = jnp.exp(L - lw)
        p_inv = jnp.exp(-L)
        p_end = jnp.exp(LC - L)
        p_c = jnp.exp(LC)
        r, k, v, a, b = (ref[0, rows, :].astype(F32) for ref in (r_ref, k_ref, v_ref, a_ref, b_ref))
        rt = r * p_in
        at = a * p_ex
        bt = b * p_inv
        kt = k * p_inv
        bh = b * p_end
        kh = k * p_end
        ats, rts, bts, kts, vs, bhs, khs = ([stack(x[:, sl]) for sl in sls] for x in (at, rt, bt, kt, v, bh, kh))
        amat = [_mm(jnp.concatenate([ats[p], rts[p]], 0), jnp.concatenate([bts[p], kts[p]], 0), _NT) for p in pairs]
        aab = [jnp.where(strict, amat[p][:pair, :pair], 0.0) for p in pairs]
        aak = [jnp.where(strict, amat[p][:pair, pair:], 0.0) for p in pairs]
        arb = [jnp.where(incl, amat[p][pair:, :pair], 0.0) for p in pairs]
        ark = [jnp.where(incl, amat[p][pair:, pair:], 0.0) for p in pairs]
        t_inv = [eye_f + aab[p] for p in pairs]
        aj = [_mm(aab[p], aab[p], _NN) for p in pairs]
        w1 = [_mm(aak[p], vs[p], _NN) for p in pairs]
        n_factors = C.bit_length() - 1
        for _ in range(n_factors - 2):
            res = [_mm(aj[p], jnp.concatenate([t_inv[p], aj[p]], 1), _NN) for p in pairs]
            t_inv = [t_inv[p] + res[p][:, :pair] for p in pairs]
            aj = [res[p][:, pair:] for p in pairs]
        t_inv = [t_inv[p] + _mm(aj[p], t_inv[p], _NN) for p in pairs]
        x = [_mm(t_inv[p], jnp.concatenate([ats[p], w1[p]], 1), _NN) for p in pairs]
        xv = [jnp.concatenate([x[p], jnp.concatenate([zero, vs[p]], 1)], 0) for p in pairs]
        z = [_mm(jnp.concatenate([arb[p], ark[p]], 1), xv[p], _NN) for p in pairs]
        gmat = [_mm(jnp.concatenate([bhs[p], khs[p]], 0), xv[p], _TN) for p in pairs]
        st1 = []
        for p in pairs:
            mt = jnp.where(eye, p_c[:, sls[p]], 0.0) + gmat[p][:, :pair]
            st1.append(_mm(mt, st0[p], _NN) + gmat[p][:, pair:])
        for p in pairs:
            rp_s = rts[p] + z[p][:, :pair]
            yv_s = z[p][:, pair:]
            rp = rp_s[:C] + rp_s[C:]
            yv = yv_s[:C] + yv_s[C:]
            y_ref[0, rows, sls[p]] = _mm(rp, st0[p], _NN) + yv
        return st1

    st = [st_ref[p] for p in pairs]
    for q in range(lw_ref.shape[1] // C):
        st = one_chunk(slice(q * C, (q + 1) * C), st)
    for p in pairs:
        st_ref[p] = st[p]


def _wkv_scan(r, lw, k, v, a, b, *, head):
    bsz, seq, n = r.shape
    rows = SCAN_CHUNK * SCAN_CHUNKS_PER_STEP
    spec = pl.BlockSpec((1, rows, n), lambda i, c: (i, c, 0))
    return pl.pallas_call(
        functools.partial(_scan_kernel, head=head, chunk=SCAN_CHUNK),
        grid=(bsz, seq // rows),
        in_specs=[spec] * 6,
        out_specs=spec,
        out_shape=jax.ShapeDtypeStruct((bsz, seq, n), F32),
        scratch_shapes=[pltpu.VMEM((n // (2 * head), 2 * head, 2 * head), F32)],
        compiler_params=_cparams(("arbitrary", "arbitrary"), 48),
        name="wkv_scan",
    )(r, lw, k, v, a, b)


def _post_kernel(y_ref, r_ref, k_ref, v_ref, g_ref, rk_ref, lnw_ref, lnb_ref, bd_ref, wpb_ref, gate_ref,
                 pa_ref, wout_ref, x_ref, n2_ref, x1_o, h2_o, *, head):
    bd = bd_ref[...]
    y = y_ref[...]
    inv = 1.0 / head
    mean = _head_sum(y, bd) * inv
    d = y - mean
    var = _head_sum(d * d, bd) * inv
    yn = d * lax.rsqrt(var + GN_EPS) * lnw_ref[...] + lnb_ref[...]
    r, k, v = (ref[...].astype(F32) for ref in (r_ref, k_ref, v_ref))
    bonus = _head_sum(r * k * rk_ref[...], bd) * v
    yb = ((yn + bonus) * g_ref[...].astype(F32)).astype(BF16)
    pb = jnp.dot(yb, wpb_ref[...], preferred_element_type=F32) * gate_ref[...].astype(F32)
    merged = (pa_ref[...].astype(F32) + pb).astype(BF16)
    x1 = x_ref[...] + jnp.dot(merged, wout_ref[...], preferred_element_type=F32)
    x1_o[...] = x1
    h2_o[...] = _rmsnorm(x1, n2_ref[...]).astype(h2_o.dtype)


def _rwkv_post(y, r, k, v, g, rk, lnw, lnb, bd, wpb, gates, pa, wout, x2, n2, *, head, tm):
    m, db = y.shape
    dm = x2.shape[1]
    row = lambda n, j=0: pl.BlockSpec((tm, n), lambda i: (i, j))
    vec = lambda n: pl.BlockSpec((1, n), lambda i: (0, 0))
    full = lambda a: pl.BlockSpec(a.shape, lambda i: (0, 0), pipeline_mode=pl.Buffered(1))
    return pl.pallas_call(
        functools.partial(_post_kernel, head=head),
        grid=(m // tm,),
        in_specs=[row(db), row(db), row(db), row(db), row(db), vec(db), vec(db), vec(db), full(bd), full(wpb),
                  row(dm, 1), row(dm), full(wout), row(dm), vec(dm)],
        out_specs=[row(dm), row(dm)],
        out_shape=[jax.ShapeDtypeStruct((m, dm), F32), jax.ShapeDtypeStruct((m, dm), BF16)],
        compiler_params=_cparams(("parallel",), 56),
        name="rwkv_post",
    )(y, r, k, v, g, rk, lnw, lnb, bd, wpb, gates, pa, wout, x2, n2)


def _ffn_kernel(h_ref, hp_ref, x1_ref, wg_ref, wv_ref, cwg_ref, cwv_ref, cbg_ref, cbv_ref, wd_ref, nf_ref,
                o_ref, hext_ref, ug_ref, uv_ref, acc_ref, *, seq, taps):
    tm = h_ref.shape[0]
    halo = hp_ref.shape[0]
    i = pl.program_id(0)
    j = pl.program_id(1)

    @pl.when(j == 0)
    def _():
        first = (i * tm) % seq == 0
        hext_ref[0:halo, :] = jnp.where(first, jnp.zeros_like(hp_ref), hp_ref[...])
        hext_ref[halo:, :] = h_ref[...]
        acc_ref[...] = jnp.zeros_like(acc_ref)

    hext = hext_ref[...]
    ug_ref[...] = jnp.dot(hext, wg_ref[...], preferred_element_type=F32)
    uv_ref[...] = jnp.dot(hext, wv_ref[...], preferred_element_type=F32)

    def conv(u_ref, cw_ref, cb_ref):
        out = cb_ref[...]
        for t in range(taps):
            off = halo - (taps - 1) + t
            out = out + cw_ref[t:t + 1, :] * u_ref[off:off + tm, :]
        return out

    act = (_gelu(conv(ug_ref, cwg_ref, cbg_ref)) * conv(uv_ref, cwv_ref, cbv_ref)).astype(BF16)
    acc_ref[...] += jnp.dot(act, wd_ref[...], preferred_element_type=F32)

    @pl.when(j == pl.num_programs(1) - 1)
    def _():
        o_ref[...] = _rmsnorm(x1_ref[...] + acc_ref[...], nf_ref[...])


def _conv_ffn(h2, x1, wup, cw, cb, wd, nf, *, seq, tm, tn):
    m, dm = h2.shape
    dff = wd.shape[0]
    nj = dff // tn
    taps = cw.shape[0]
    halo = SUBLANES_BF16
    blocks_per_tile = tm // halo
    return pl.pallas_call(
        functools.partial(_ffn_kernel, seq=seq, taps=taps),
        grid=(m // tm, nj),
        in_specs=[
            pl.BlockSpec((tm, dm), lambda i, j: (i, 0)),
            pl.BlockSpec((halo, dm), lambda i, j: (jnp.maximum(i * blocks_per_tile - 1, 0), 0)),
            pl.BlockSpec((tm, dm), lambda i, j: (i, 0)),
            pl.BlockSpec((dm, tn), lambda i, j: (0, j)),
            pl.BlockSpec((dm, tn), lambda i, j: (0, nj + j)),
            pl.BlockSpec((taps, tn), lambda i, j: (0, j)),
            pl.BlockSpec((taps, tn), lambda i, j: (0, nj + j)),
            pl.BlockSpec((1, tn), lambda i, j: (0, j)),
            pl.BlockSpec((1, tn), lambda i, j: (0, nj + j)),
            pl.BlockSpec((tn, dm), lambda i, j: (j, 0)),
            pl.BlockSpec((1, dm), lambda i, j: (0, 0)),
        ],
        out_specs=pl.BlockSpec((tm, dm), lambda i, j: (i, 0)),
        out_shape=jax.ShapeDtypeStruct((m, dm), F32),
        scratch_shapes=[
            pltpu.VMEM((tm + halo, dm), BF16),
            pltpu.VMEM((tm + halo, tn), F32),
            pltpu.VMEM((tm + halo, tn), F32),
            pltpu.VMEM((tm, dm), F32),
        ],
        compiler_params=_cparams(("parallel", "arbitrary"), 56),
        name="conv_ffn",
    )(h2, h2, x1, wup, wup, cw, cw, cb, cb, wd, nf)


def _pick_tile(n, want):
    t = min(want, n)
    while n % t:
        t //= 2
    return t


def _block_diag_ones(width, block):
    idx = jnp.arange(width) // block
    return (idx[:, None] == idx[None, :]).astype(BF16)


def _layer(x2, seq, p):
    m, dm = x2.shape
    da = p["gmlp_ln_w"].shape[0]
    db = p["rwkv_w0"].shape[0]
    n_heads, head = p["rwkv_rk"].shape
    lora_w = p["rwkv_w2"].shape[0]
    lora_a = p["rwkv_a2"].shape[0]
    lora_g = p["rwkv_g2"].shape[0]
    assert lora_w + lora_a == LANES and lora_g <= MXU_WIDTH
    d_b_in = 3 * db + lora_w + lora_a + lora_g
    w_in = p["w_in"]
    nf_pad = 3 * db + LANES + MXU_WIDTH
    tn_in = 4 * MXU_WIDTH
    nf_store = -(-nf_pad // tn_in) * tn_in
    w_all = jnp.concatenate([
        w_in[:, :2 * da].astype(BF16),
        jnp.pad(w_in[:, 2 * da:2 * da + d_b_in].astype(BF16), ((0, 0), (0, nf_store - d_b_in))),
        w_in[:, 2 * da + d_b_in:].astype(BF16)], axis=1)
    uv, feat, gates = _inproj(x2, p["norm1_g"].reshape(1, dm), w_all, widths=(2 * da, nf_store, 2 * dm),
                              tm=_pick_tile(m, 1024), tn=tn_in)

    pa = _gmlp(uv, gates, p["gmlp_ln_w"].reshape(1, da), p["gmlp_ln_b"].reshape(1, da), p["gmlp_ws"],
               p["gmlp_bs"].T, p["w_proj_a"].astype(BF16), tm=_pick_tile(seq, 512))

    mu = jnp.pad(p["mu_b"], (0, nf_pad - d_b_in)).reshape(1, nf_pad)
    zeros = jnp.zeros((lora_w, db), F32)
    w01 = jnp.concatenate([jnp.concatenate([p["rwkv_w2"], zeros], 1),
                           jnp.concatenate([jnp.zeros((lora_a, db), F32), p["rwkv_a2"]], 1)], 0).astype(BF16)
    g2p = jnp.pad(p["rwkv_g2"], ((0, MXU_WIDTH - lora_g), (0, 0))).astype(BF16)
    bd = _block_diag_ones(MXU_WIDTH, head)
    vecb = lambda a: a.reshape(1, db)
    r, lw, k, v, av, bv, g = _rwkv_prep(
        feat, mu, vecb(p["rwkv_w0"]), vecb(p["rwkv_a0"]), vecb(p["rwkv_kk"]), vecb(p["rwkv_ka"]), w01, g2p, bd,
        seq=seq, tm=_pick_tile(seq, 512), lora_wa=lora_w)
    bsz = m // seq
    to3 = lambda t: t.reshape(bsz, seq, db)
    y = _wkv_scan(to3(r), to3(lw), to3(k), to3(v), to3(av), to3(bv), head=head).reshape(m, db)
    x1, h2 = _rwkv_post(y, r, k, v, g, vecb(p["rwkv_rk"]), vecb(p["rwkv_ln_w"]), vecb(p["rwkv_ln_b"]), bd,
                        p["w_proj_b"].astype(BF16), gates, pa, p["w_out"].astype(BF16), x2,
                        p["norm2_g"].reshape(1, dm), head=head, tm=_pick_tile(seq, 256))
    return x1, h2


def kernel(x, norm1_g, w_in, mu_b, rwkv_w0, rwkv_w2, rwkv_a0, rwkv_a2, rwkv_g2, rwkv_kk, rwkv_ka, rwkv_rk, rwkv_ln_w, rwkv_ln_b, gmlp_ln_w, gmlp_ln_b, gmlp_ws, gmlp_bs, w_proj_a, w_proj_b, w_out, norm2_g, w_up, conv_w, conv_b, w_down, norm_f_g):
    bsz, seq, dm = x.shape
    depth = w_in.shape[0]
    assert depth == 1, "the fused FFN applies the final rmsnorm; one layer is supported"
    x2 = x.reshape(bsz * seq, dm)
    l = 0
    p = dict(norm1_g=norm1_g[l], w_in=w_in[l], mu_b=mu_b[l], rwkv_w0=rwkv_w0[l], rwkv_w2=rwkv_w2[l],
             rwkv_a0=rwkv_a0[l], rwkv_a2=rwkv_a2[l], rwkv_g2=rwkv_g2[l], rwkv_kk=rwkv_kk[l], rwkv_ka=rwkv_ka[l],
             rwkv_rk=rwkv_rk[l], rwkv_ln_w=rwkv_ln_w[l], rwkv_ln_b=rwkv_ln_b[l], gmlp_ln_w=gmlp_ln_w[l],
             gmlp_ln_b=gmlp_ln_b[l], gmlp_ws=gmlp_ws[l], gmlp_bs=gmlp_bs[l], w_proj_a=w_proj_a[l],
             w_proj_b=w_proj_b[l], w_out=w_out[l], norm2_g=norm2_g[l])
    x1, h2 = _layer(x2, seq, p)
    dff = w_down.shape[1]
    out = _conv_ffn(h2, x1, w_up[l].astype(BF16), conv_w[l], conv_b[l].reshape(1, -1), w_down[l].astype(BF16),
                    norm_f_g.reshape(1, dm), seq=seq, tm=_pick_tile(seq, 512), tn=_pick_tile(dff, 512))
    return out.reshape(bsz, seq, dm)
```

```python
import functools

import jax
import jax.numpy as jnp
from jax import lax
from jax.experimental import pallas as pl
from jax.experimental.pallas import tpu as pltpu

F32 = jnp.float32
BF16 = jnp.bfloat16

RMS_EPS = 1e-6
LN_EPS = 1e-5
GN_EPS = 64e-5
KK_NORM_FLOOR = 1e-12
EXP_NEG_HALF = 0.6065306597126334

LANES = 128
MXU_WIDTH = 256
SUBLANES_F32 = 8
SUBLANES_BF16 = 16

SCAN_CHUNK = 64
SCAN_CHUNKS_PER_STEP = 8


def _cparams(semantics, vmem_mb):
    return pltpu.CompilerParams(dimension_semantics=semantics, vmem_limit_bytes=vmem_mb * 1024 * 1024)


def _rmsnorm(x, g):
    ms = jnp.mean(x * x, axis=-1, keepdims=True)
    return x * lax.rsqrt(ms + RMS_EPS) * g


def _gelu(x):
    return 0.5 * x * (1.0 + lax.erf(x * (0.5 ** 0.5)))


def _sigmoid(x):
    return 0.5 * (jnp.tanh(0.5 * x) + 1.0)


def _bdot(a, b):
    return jnp.dot(a.astype(BF16), b.astype(BF16), preferred_element_type=F32)


def _inproj_kernel(x_ref, g_ref, w_ref, uv_ref, feat_ref, gate_ref, h_ref, *, n_uv, n_feat):
    j = pl.program_id(1)

    @pl.when(j == 0)
    def _():
        h_ref[...] = _rmsnorm(x_ref[...], g_ref[...]).astype(h_ref.dtype)

    def project():
        return jnp.dot(h_ref[...], w_ref[...], preferred_element_type=F32)

    @pl.when(j < n_uv)
    def _():
        uv_ref[...] = _gelu(project()).astype(uv_ref.dtype)

    @pl.when((j >= n_uv) & (j < n_uv + n_feat))
    def _():
        feat_ref[...] = project().astype(feat_ref.dtype)

    @pl.when(j >= n_uv + n_feat)
    def _():
        gate_ref[...] = _sigmoid(project()).astype(gate_ref.dtype)


def _inproj(x2, g, w, *, widths, tm, tn):
    m, d = x2.shape
    n_uv, n_feat, n_gate = (wd // tn for wd in widths)
    assert all(wd % tn == 0 for wd in widths) and sum(widths) == w.shape[1]
    return pl.pallas_call(
        functools.partial(_inproj_kernel, n_uv=n_uv, n_feat=n_feat),
        grid=(m // tm, n_uv + n_feat + n_gate),
        in_specs=[
            pl.BlockSpec((tm, d), lambda i, j: (i, 0)),
            pl.BlockSpec((1, d), lambda i, j: (0, 0)),
            pl.BlockSpec((d, tn), lambda i, j: (0, j)),
        ],
        out_specs=[
            pl.BlockSpec((tm, tn), lambda i, j: (i, jnp.minimum(j, n_uv - 1))),
            pl.BlockSpec((tm, tn), lambda i, j: (i, jnp.clip(j - n_uv, 0, n_feat - 1))),
            pl.BlockSpec((tm, tn), lambda i, j: (i, jnp.clip(j - n_uv - n_feat, 0, n_gate - 1))),
        ],
        out_shape=[
            jax.ShapeDtypeStruct((m, widths[0]), BF16),
            jax.ShapeDtypeStruct((m, widths[1]), BF16),
            jax.ShapeDtypeStruct((m, widths[2]), BF16),
        ],
        scratch_shapes=[pltpu.VMEM((tm, d), BF16)],
        compiler_params=_cparams(("parallel", "arbitrary"), 48),
        name="inproj",
    )(x2, g, w)


def _gmlp_kernel(u_ref, v_ref, lnw_ref, lnb_ref, ws_ref, bst_ref, wpa_ref, gate_ref, o_ref, ya_ref, *, chunk):
    tm, da = v_ref.shape
    n_groups = ws_ref.shape[0]
    gw = da // n_groups
    v = v_ref[...].astype(F32)
    mu = jnp.mean(v, axis=-1, keepdims=True)
    d = v - mu
    var = jnp.mean(d * d, axis=-1, keepdims=True)
    vn = (d * lax.rsqrt(var + LN_EPS) * lnw_ref[...] + lnb_ref[...]).astype(BF16)
    row = lax.broadcasted_iota(jnp.int32, (chunk, chunk), 0)
    col = lax.broadcasted_iota(jnp.int32, (chunk, chunk), 1)
    causal = row >= col
    for g in range(n_groups):
        wm = jnp.where(causal, ws_ref[g], 0.0).astype(BF16)
        bias = bst_ref[:, g:g + 1]
        for c in range(tm // chunk):
            rs = slice(c * chunk, (c + 1) * chunk)
            cs = slice(g * gw, (g + 1) * gw)
            mixed = jnp.dot(wm, vn[rs, cs], preferred_element_type=F32) + bias
            ya_ref[rs, cs] = (u_ref[rs, cs].astype(F32) * mixed).astype(BF16)
    pa = jnp.dot(ya_ref[...], wpa_ref[...], preferred_element_type=F32)
    o_ref[...] = (pa * gate_ref[...].astype(F32)).astype(o_ref.dtype)


def _gmlp(uv, gates, lnw, lnb, ws, bst, wpa, *, tm):
    m = uv.shape[0]
    da = uv.shape[1] // 2
    dm = wpa.shape[1]
    chunk = ws.shape[1]
    return pl.pallas_call(
        functools.partial(_gmlp_kernel, chunk=chunk),
        grid=(m // tm,),
        in_specs=[
            pl.BlockSpec((tm, da), lambda i: (i, 0)),
            pl.BlockSpec((tm, da), lambda i: (i, 1)),
            pl.BlockSpec((1, da), lambda i: (0, 0)),
            pl.BlockSpec((1, da), lambda i: (0, 0)),
            pl.BlockSpec(ws.shape, lambda i: (0, 0, 0)),
            pl.BlockSpec(bst.shape, lambda i: (0, 0)),
            pl.BlockSpec(wpa.shape, lambda i: (0, 0), pipeline_mode=pl.Buffered(1)),
            pl.BlockSpec((tm, dm), lambda i: (i, 0)),
        ],
        out_specs=pl.BlockSpec((tm, dm), lambda i: (i, 0)),
        out_shape=jax.ShapeDtypeStruct((m, dm), BF16),
        scratch_shapes=[pltpu.VMEM((tm, da), BF16)],
        compiler_params=_cparams(("parallel",), 48),
        name="gmlp",
    )(uv, uv, lnw, lnb, ws, bst, wpa, gates)


def _head_sum(x, bd):
    xb = x.astype(BF16)
    outs = [jnp.dot(xb[:, j:j + MXU_WIDTH], bd, preferred_element_type=F32) for j in range(0, x.shape[1], MXU_WIDTH)]
    return jnp.concatenate(outs, axis=1)


def _prep_kernel(feat_ref, prev_ref, mu_ref, w0_ref, a0_ref, kk_ref, ka_ref, w01_ref, g2_ref, bd_ref,
                 r_o, lw_o, k_o, v_o, a_o, b_o, g_o, *, seq, db, lora_wa):
    tm = feat_ref.shape[0]
    i = pl.program_id(0)
    feat = feat_ref[...].astype(F32)
    first = (i * tm) % seq == 0
    halo = prev_ref.shape[0]
    prev = jnp.where(first, 0.0, prev_ref[halo - 1:halo, :].astype(F32))
    rows = lax.broadcasted_iota(jnp.int32, (tm, 1), 0)
    shifted = jnp.where(rows == 0, prev, pltpu.roll(feat, 1, 0))
    fm = feat + (shifted - feat) * mu_ref[...]
    r = fm[:, 0:db]
    k = fm[:, db:2 * db]
    v = fm[:, 2 * db:3 * db]
    l0 = fm[:, 3 * db:3 * db + LANES]
    l1 = fm[:, 3 * db + LANES:]
    lane = lax.broadcasted_iota(jnp.int32, l0.shape, 1)
    x0 = jnp.where(lane < lora_wa, jnp.tanh(l0), l0)
    wa = _bdot(x0, w01_ref[...])
    g = _bdot(_sigmoid(l1), g2_ref[...])
    w_raw = w0_ref[...] + wa[:, :db]
    a = _sigmoid(a0_ref[...] + wa[:, db:])
    lw = -EXP_NEG_HALF * _sigmoid(w_raw)
    kkv = k * kk_ref[...]
    kkn = kkv * lax.rsqrt(jnp.maximum(_head_sum(kkv * kkv, bd_ref[...]), KK_NORM_FLOOR ** 2))
    kp = k * (1.0 + (a - 1.0) * ka_ref[...])
    lw_o[...] = lw
    for o_ref, val in ((r_o, r), (k_o, kp), (v_o, v), (a_o, -kkn), (b_o, kkn * a), (g_o, g)):
        o_ref[...] = val.astype(o_ref.dtype)


def _rwkv_prep(feat, mu, w0, a0, kk, ka, w01, g2p, bd, *, seq, tm, lora_wa):
    m = feat.shape[0]
    nf = mu.shape[1]
    db = w0.shape[1]
    vec = lambda n: pl.BlockSpec((1, n), lambda i: (0, 0))
    full = lambda a: pl.BlockSpec(a.shape, lambda i: (0, 0))
    out_spec = pl.BlockSpec((tm, db), lambda i: (i, 0))
    halo = SUBLANES_BF16 if feat.dtype == BF16 else SUBLANES_F32
    blocks_per_tile = tm // halo
    return pl.pallas_call(
        functools.partial(_prep_kernel, seq=seq, db=db, lora_wa=lora_wa),
        grid=(m // tm,),
        in_specs=[
            pl.BlockSpec((tm, nf), lambda i: (i, 0)),
            pl.BlockSpec((halo, nf), lambda i: (jnp.maximum(i * blocks_per_tile - 1, 0), 0)),
            vec(nf), vec(db), vec(db), vec(db), vec(db), full(w01), full(g2p), full(bd),
        ],
        out_specs=[out_spec] * 7,
        out_shape=[jax.ShapeDtypeStruct((m, db), F32 if i == 1 else BF16) for i in range(7)],
        compiler_params=_cparams(("parallel",), 48),
        name="rwkv_prep",
    )(feat, feat, mu, w0, a0, kk, ka, w01, g2p, bd)


def _mm(a, b, dims):
    return lax.dot_general(a.astype(BF16), b.astype(BF16), (dims, ((), ())), preferred_element_type=F32)


_NN = ((1,), (0,))
_NT = ((1,), (1,))
_TN = ((0,), (0,))


def _scan_kernel(r_ref, lw_ref, k_ref, v_ref, a_ref, b_ref, y_ref, st_ref, *, head, chunk):
    c = pl.program_id(1)
    C = chunk
    n = lw_ref.shape[2]
    pair = 2 * head
    assert pair == LANES and 2 * C == LANES
    pairs = range(n // pair)
    sls = [slice(p * pair, (p + 1) * pair) for p in pairs]

    @pl.when(c == 0)
    def _():
        st_ref[...] = jnp.zeros_like(st_ref)

    rc = lax.broadcasted_iota(jnp.int32, (C, C), 0)
    cc = lax.broadcasted_iota(jnp.int32, (C, C), 1)
    tri = (rc >= cc).astype(F32)
    lane = lax.broadcasted_iota(jnp.int32, (C, pair), 1)
    m0 = lane < head
    ri = lax.broadcasted_iota(jnp.int32, (pair, pair), 0)
    ci = lax.broadcasted_iota(jnp.int32, (pair, pair), 1)
    same = (ri >= C) == (ci >= C)
    strict = same & (ri > ci)
    incl = same & (ri >= ci)
    eye = ri == ci
    eye_f = eye.astype(F32)
    zero = jnp.zeros((pair, pair), F32)

    def stack(x):
        return jnp.concatenate([jnp.where(m0, x, 0.0), jnp.where(m0, 0.0, x)], axis=0)

    def one_chunk(rows, st0):
        lw = lw_ref[0, rows, :]
        L = lax.dot_general(tri, lw, (_NN, ((), ())), precision=lax.Precision.HIGHEST, preferred_element_type=F32)
        LC = L[C - 1:C, :]
        p_in = jnp.exp(L)
        p_ex = jnp.exp(L - lw)
        p_inv = jnp.exp(-L)
        p_end = jnp.exp(LC - L)
        p_c = jnp.exp(LC)
        r, k, v, a, b = (ref[0, rows, :].astype(F32) for ref in (r_ref, k_ref, v_ref, a_ref, b_ref))
        rt = r * p_in
        at = a * p_ex
        bt = b * p_inv
        kt = k * p_inv
        bh = b * p_end
        kh = k * p_end
        ats, rts, bts, kts, vs, bhs, khs = ([stack(x[:, sl]) for sl in sls] for x in (at, rt, bt, kt, v, bh, kh))
        amat = [_mm(jnp.concatenate([ats[p], rts[p]], 0), jnp.concatenate([bts[p], kts[p]], 0), _NT) for p in pairs]
        aab = [jnp.where(strict, amat[p][:pair, :pair], 0.0) for p in pairs]
        aak = [jnp.where(strict, amat[p][:pair, pair:], 0.0) for p in pairs]
        arb = [jnp.where(incl, amat[p][pair:, :pair], 0.0) for p in pairs]
        ark = [jnp.where(incl, amat[p][pair:, pair:], 0.0) for p in pairs]
        t_inv = [eye_f + aab[p] for p in pairs]
        aj = [_mm(aab[p], aab[p], _NN) for p in pairs]
        w1 = [_mm(aak[p], vs[p], _NN) for p in pairs]
        n_factors = C.bit_length() - 1
        for _ in range(n_factors - 2):
            res = [_mm(aj[p], jnp.concatenate([t_inv[p], aj[p]], 1), _NN) for p in pairs]
            t_inv = [t_inv[p] + res[p][:, :pair] for p in pairs]
            aj = [res[p][:, pair:] for p in pairs]
        t_inv = [t_inv[p] + _mm(aj[p], t_inv[p], _NN) for p in pairs]
        x = [_mm(t_inv[p], jnp.concatenate([ats[p], w1[p]], 1), _NN) for p in pairs]
        xv = [jnp.concatenate([x[p], jnp.concatenate([zero, vs[p]], 1)], 0) for p in pairs]
        z = [_mm(jnp.concatenate([arb[p], ark[p]], 1), xv[p], _NN) for p in pairs]
        gmat = [_mm(jnp.concatenate([bhs[p], khs[p]], 0), xv[p], _TN) for p in pairs]
        st1 = []
        for p in pairs:
            mt = jnp.where(eye, p_c[:, sls[p]], 0.0) + gmat[p][:, :pair]
            st1.append(_mm(mt, st0[p], _NN) + gmat[p][:, pair:])
        for p in pairs:
            rp_s = rts[p] + z[p][:, :pair]
            yv_s = z[p][:, pair:]
            rp = rp_s[:C] + rp_s[C:]
            yv = yv_s[:C] + yv_s[C:]
            y_ref[0, rows, sls[p]] = _mm(rp, st0[p], _NN) + yv
        return st1

    st = [st_ref[p] for p in pairs]
    for q in range(lw_ref.shape[1] // C):
        st = one_chunk(slice(q * C, (q + 1) * C), st)
    for p in pairs:
        st_ref[p] = st[p]


def _wkv_scan(r, lw, k, v, a, b, *, head):
    bsz, seq, n = r.shape
    rows = SCAN_CHUNK * SCAN_CHUNKS_PER_STEP
    spec = pl.BlockSpec((1, rows, n), lambda i, c: (i, c, 0))
    return pl.pallas_call(
        functools.partial(_scan_kernel, head=head, chunk=SCAN_CHUNK),
        grid=(bsz, seq // rows),
        in_specs=[spec] * 6,
        out_specs=spec,
        out_shape=jax.ShapeDtypeStruct((bsz, seq, n), F32),
        scratch_shapes=[pltpu.VMEM((n // (2 * head), 2 * head, 2 * head), F32)],
        compiler_params=_cparams(("arbitrary", "arbitrary"), 48),
        name="wkv_scan",
    )(r, lw, k, v, a, b)


def _post_kernel(y_ref, r_ref, k_ref, v_ref, g_ref, rk_ref, lnw_ref, lnb_ref, bd_ref, wpb_ref, gate_ref,
                 pa_ref, wout_ref, x_ref, n2_ref, x1_o, h2_o, *, head):
    bd = bd_ref[...]
    y = y_ref[...]
    inv = 1.0 / head
    mean = _head_sum(y, bd) * inv
    d = y - mean
    var = _head_sum(d * d, bd) * inv
    yn = d * lax.rsqrt(var + GN_EPS) * lnw_ref[...] + lnb_ref[...]
    r, k, v = (ref[...].astype(F32) for ref in (r_ref, k_ref, v_ref))
    bonus = _head_sum(r * k * rk_ref[...], bd) * v
    yb = ((yn + bonus) * g_ref[...].astype(F32)).astype(BF16)
    pb = jnp.dot(yb, wpb_ref[...], preferred_element_type=F32) * gate_ref[...].astype(F32)
    merged = (pa_ref[...].astype(F32) + pb).astype(BF16)
    x1 = x_ref[...] + jnp.dot(merged, wout_ref[...], preferred_element_type=F32)
    x1_o[...] = x1
    h2_o[...] = _rmsnorm(x1, n2_ref[...]).astype(h2_o.dtype)


def _rwkv_post(y, r, k, v, g, rk, lnw, lnb, bd, wpb, gates, pa, wout, x2, n2, *, head, tm):
    m, db = y.shape
    dm = x2.shape[1]
    row = lambda n, j=0: pl.BlockSpec((tm, n), lambda i: (i, j))
    vec = lambda n: pl.BlockSpec((1, n), lambda i: (0, 0))
    full = lambda a: pl.BlockSpec(a.shape, lambda i: (0, 0), pipeline_mode=pl.Buffered(1))
    return pl.pallas_call(
        functools.partial(_post_kernel, head=head),
        grid=(m // tm,),
        in_specs=[row(db), row(db), row(db), row(db), row(db), vec(db), vec(db), vec(db), full(bd), full(wpb),
                  row(dm, 1), row(dm), full(wout), row(dm), vec(dm)],
        out_specs=[row(dm), row(dm)],
        out_shape=[jax.ShapeDtypeStruct((m, dm), F32), jax.ShapeDtypeStruct((m, dm), BF16)],
        compiler_params=_cparams(("parallel",), 60),
        name="rwkv_post",
    )(y, r, k, v, g, rk, lnw, lnb, bd, wpb, gates, pa, wout, x2, n2)


def _ffn_kernel(h_ref, hp_ref, x1_ref, wg_ref, wv_ref, cwg_ref, cwv_ref, cbg_ref, cbv_ref, wd_ref, nf_ref,
                o_ref, hext_ref, ug_ref, uv_ref, acc_ref, *, seq, taps):
    tm = h_ref.shape[0]
    halo = hp_ref.shape[0]
    i = pl.program_id(0)
    j = pl.program_id(1)

    @pl.when(j == 0)
    def _():
        first = (i * tm) % seq == 0
        hext_ref[0:halo, :] = jnp.where(first, jnp.zeros_like(hp_ref), hp_ref[...])
        hext_ref[halo:, :] = h_ref[...]
        acc_ref[...] = jnp.zeros_like(acc_ref)

    hext = hext_ref[...]
    ug_ref[...] = jnp.dot(hext, wg_ref[...], preferred_element_type=F32)
    uv_ref[...] = jnp.dot(hext, wv_ref[...], preferred_element_type=F32)

    def conv(u_ref, cw_ref, cb_ref):
        out = cb_ref[...]
        for t in range(taps):
            off = halo - (taps - 1) + t
            out = out + cw_ref[t:t + 1, :] * u_ref[off:off + tm, :]
        return out

    act = (_gelu(conv(ug_ref, cwg_ref, cbg_ref)) * conv(uv_ref, cwv_ref, cbv_ref)).astype(BF16)
    acc_ref[...] += jnp.dot(act, wd_ref[...], preferred_element_type=F32)

    @pl.when(j == pl.num_programs(1) - 1)
    def _():
        o_ref[...] = _rmsnorm(x1_ref[...] + acc_ref[...], nf_ref[...])


def _conv_ffn(h2, x1, wup, cw, cb, wd, nf, *, seq, tm, tn):
    m, dm = h2.shape
    dff = wd.shape[0]
    nj = dff // tn
    taps = cw.shape[0]
    halo = SUBLANES_BF16
    blocks_per_tile = tm // halo
    return pl.pallas_call(
        functools.partial(_ffn_kernel, seq=seq, taps=taps),
        grid=(m // tm, nj),
        in_specs=[
            pl.BlockSpec((tm, dm), lambda i, j: (i, 0)),
            pl.BlockSpec((halo, dm), lambda i, j: (jnp.maximum(i * blocks_per_tile - 1, 0), 0)),
            pl.BlockSpec((tm, dm), lambda i, j: (i, 0)),
            pl.BlockSpec((dm, tn), lambda i, j: (0, j)),
            pl.BlockSpec((dm, tn), lambda i, j: (0, nj + j)),
            pl.BlockSpec((taps, tn), lambda i, j: (0, j)),
            pl.BlockSpec((taps, tn), lambda i, j: (0, nj + j)),
            pl.BlockSpec((1, tn), lambda i, j: (0, j)),
            pl.BlockSpec((1, tn), lambda i, j: (0, nj + j)),
            pl.BlockSpec((tn, dm), lambda i, j: (j, 0)),
            pl.BlockSpec((1, dm), lambda i, j: (0, 0)),
        ],
        out_specs=pl.BlockSpec((tm, dm), lambda i, j: (i, 0)),
        out_shape=jax.ShapeDtypeStruct((m, dm), F32),
        scratch_shapes=[
            pltpu.VMEM((tm + halo, dm), BF16),
            pltpu.VMEM((tm + halo, tn), F32),
            pltpu.VMEM((tm + halo, tn), F32),
            pltpu.VMEM((tm, dm), F32),
        ],
        compiler_params=_cparams(("parallel", "arbitrary"), 56),
        name="conv_ffn",
    )(h2, h2, x1, wup, wup, cw, cw, cb, cb, wd, nf)


def _pick_tile(n, want):
    t = min(want, n)
    while n % t:
        t //= 2
    return t


def _block_diag_ones(width, block):
    idx = jnp.arange(width) // block
    return (idx[:, None] == idx[None, :]).astype(BF16)


def _layer(x2, seq, p):
    m, dm = x2.shape
    da = p["gmlp_ln_w"].shape[0]
    db = p["rwkv_w0"].shape[0]
    n_heads, head = p["rwkv_rk"].shape
    lora_w = p["rwkv_w2"].shape[0]
    lora_a = p["rwkv_a2"].shape[0]
    lora_g = p["rwkv_g2"].shape[0]
    assert lora_w + lora_a == LANES and lora_g <= MXU_WIDTH
    d_b_in = 3 * db + lora_w + lora_a + lora_g
    w_in = p["w_in"]
    nf_pad = 3 * db + LANES + MXU_WIDTH
    tn_in = 4 * MXU_WIDTH
    nf_store = -(-nf_pad // tn_in) * tn_in
    w_all = jnp.concatenate([
        w_in[:, :2 * da].astype(BF16),
        jnp.pad(w_in[:, 2 * da:2 * da + d_b_in].astype(BF16), ((0, 0), (0, nf_store - d_b_in))),
        w_in[:, 2 * da + d_b_in:].astype(BF16)], axis=1)
    uv, feat, gates = _inproj(x2, p["norm1_g"].reshape(1, dm), w_all, widths=(2 * da, nf_store, 2 * dm),
                              tm=_pick_tile(m, 1024), tn=tn_in)

    pa = _gmlp(uv, gates, p["gmlp_ln_w"].reshape(1, da), p["gmlp_ln_b"].reshape(1, da), p["gmlp_ws"],
               p["gmlp_bs"].T, p["w_proj_a"].astype(BF16), tm=_pick_tile(seq, 1024))

    mu = jnp.pad(p["mu_b"], (0, nf_pad - d_b_in)).reshape(1, nf_pad)
    zeros = jnp.zeros((lora_w, db), F32)
    w01 = jnp.concatenate([jnp.concatenate([p["rwkv_w2"], zeros], 1),
                           jnp.concatenate([jnp.zeros((lora_a, db), F32), p["rwkv_a2"]], 1)], 0).astype(BF16)
    g2p = jnp.pad(p["rwkv_g2"], ((0, MXU_WIDTH - lora_g), (0, 0))).astype(BF16)
    bd = _block_diag_ones(MXU_WIDTH, head)
    vecb = lambda a: a.reshape(1, db)
    r, lw, k, v, av, bv, g = _rwkv_prep(
        feat, mu, vecb(p["rwkv_w0"]), vecb(p["rwkv_a0"]), vecb(p["rwkv_kk"]), vecb(p["rwkv_ka"]), w01, g2p, bd,
        seq=seq, tm=_pick_tile(seq, 512), lora_wa=lora_w)
    bsz = m // seq
    to3 = lambda t: t.reshape(bsz, seq, db)
    y = _wkv_scan(to3(r), to3(lw), to3(k), to3(v), to3(av), to3(bv), head=head).reshape(m, db)
    x1, h2 = _rwkv_post(y, r, k, v, g, vecb(p["rwkv_rk"]), vecb(p["rwkv_ln_w"]), vecb(p["rwkv_ln_b"]), bd,
                        p["w_proj_b"].astype(BF16), gates, pa, p["w_out"].astype(BF16), x2,
                        p["norm2_g"].reshape(1, dm), head=head, tm=_pick_tile(seq, 512))
    return x1, h2


def kernel(x, norm1_g, w_in, mu_b, rwkv_w0, rwkv_w2, rwkv_a0, rwkv_a2, rwkv_g2, rwkv_kk, rwkv_ka, rwkv_rk, rwkv_ln_w, rwkv_ln_b, gmlp_ln_w, gmlp_ln_b, gmlp_ws, gmlp_bs, w_proj_a, w_proj_b, w_out, norm2_g, w_up, conv_w, conv_b, w_down, norm_f_g):
    bsz, seq, dm = x.shape
    depth = w_in.shape[0]
    assert depth == 1, "the fused FFN applies the final rmsnorm; one layer is supported"
    x2 = x.reshape(bsz * seq, dm)
    l = 0
    p = dict(norm1_g=norm1_g[l], w_in=w_in[l], mu_b=mu_b[l], rwkv_w0=rwkv_w0[l], rwkv_w2=rwkv_w2[l],
             rwkv_a0=rwkv_a0[l], rwkv_a2=rwkv_a2[l], rwkv_g2=rwkv_g2[l], rwkv_kk=rwkv_kk[l], rwkv_ka=rwkv_ka[l],
             rwkv_rk=rwkv_rk[l], rwkv_ln_w=rwkv_ln_w[l], rwkv_ln_b=rwkv_ln_b[l], gmlp_ln_w=gmlp_ln_w[l],
             gmlp_ln_b=gmlp_ln_b[l], gmlp_ws=gmlp_ws[l], gmlp_bs=gmlp_bs[l], w_proj_a=w_proj_a[l],
             w_proj_b=w_proj_b[l], w_out=w_out[l], norm2_g=norm2_g[l])
    x1, h2 = _layer(x2, seq, p)
    dff = w_down.shape[1]
    out = _conv_ffn(h2, x1, w_up[l].astype(BF16), conv_w[l], conv_b[l].reshape(1, -1), w_down[l].astype(BF16),
                    norm_f_g.reshape(1, dm), seq=seq, tm=_pick_tile(seq, 512), tn=_pick_tile(dff, 512))
    return out.reshape(bsz, seq, dm)
```

```python
import functools

import jax
import jax.numpy as jnp
from jax import lax
from jax.experimental import pallas as pl
from jax.experimental.pallas import tpu as pltpu

F32 = jnp.float32
BF16 = jnp.bfloat16

RMS_EPS = 1e-6
LN_EPS = 1e-5
GN_EPS = 64e-5
KK_NORM_FLOOR = 1e-12
EXP_NEG_HALF = 0.6065306597126334

LANES = 128
MXU_WIDTH = 256
SUBLANES_F32 = 8
SUBLANES_BF16 = 16

SCAN_CHUNK = 64
SCAN_CHUNKS_PER_STEP = 8


def _cparams(semantics, vmem_mb):
    return pltpu.CompilerParams(dimension_semantics=semantics, vmem_limit_bytes=vmem_mb * 1024 * 1024)


def _rmsnorm(x, g):
    ms = jnp.mean(x * x, axis=-1, keepdims=True)
    return x * lax.rsqrt(ms + RMS_EPS) * g


def _gelu(x):
    return 0.5 * x * (1.0 + lax.erf(x * (0.5 ** 0.5)))


def _sigmoid(x):
    return 0.5 * (jnp.tanh(0.5 * x) + 1.0)


def _bdot(a, b):
    return jnp.dot(a.astype(BF16), b.astype(BF16), preferred_element_type=F32)


def _inproj_kernel(x_ref, g_ref, w_ref, wl_ref, uv_ref, feat_ref, gate_ref, lora_ref, h_ref, *, n_uv, n_feat):
    j = pl.program_id(1)

    @pl.when(j == 0)
    def _():
        h_ref[...] = _rmsnorm(x_ref[...], g_ref[...]).astype(h_ref.dtype)
        lora_ref[...] = jnp.dot(h_ref[...], wl_ref[...], preferred_element_type=F32).astype(lora_ref.dtype)

    def project():
        return jnp.dot(h_ref[...], w_ref[...], preferred_element_type=F32)

    @pl.when(j < n_uv)
    def _():
        uv_ref[...] = _gelu(project()).astype(uv_ref.dtype)

    @pl.when((j >= n_uv) & (j < n_uv + n_feat))
    def _():
        feat_ref[...] = project().astype(feat_ref.dtype)

    @pl.when(j >= n_uv + n_feat)
    def _():
        gate_ref[...] = _sigmoid(project()).astype(gate_ref.dtype)


def _inproj(x2, g, w, w_lora, *, widths, tm, tn):
    m, d = x2.shape
    n_lora = w_lora.shape[1]
    n_uv, n_feat, n_gate = (wd // tn for wd in widths)
    assert all(wd % tn == 0 for wd in widths) and sum(widths) == w.shape[1]
    return pl.pallas_call(
        functools.partial(_inproj_kernel, n_uv=n_uv, n_feat=n_feat),
        grid=(m // tm, n_uv + n_feat + n_gate),
        in_specs=[
            pl.BlockSpec((tm, d), lambda i, j: (i, 0)),
            pl.BlockSpec((1, d), lambda i, j: (0, 0)),
            pl.BlockSpec((d, tn), lambda i, j: (0, j)),
            pl.BlockSpec((d, n_lora), lambda i, j: (0, 0), pipeline_mode=pl.Buffered(1)),
        ],
        out_specs=[
            pl.BlockSpec((tm, tn), lambda i, j: (i, jnp.minimum(j, n_uv - 1))),
            pl.BlockSpec((tm, tn), lambda i, j: (i, jnp.clip(j - n_uv, 0, n_feat - 1))),
            pl.BlockSpec((tm, tn), lambda i, j: (i, jnp.clip(j - n_uv - n_feat, 0, n_gate - 1))),
            pl.BlockSpec((tm, n_lora), lambda i, j: (i, 0)),
        ],
        out_shape=[
            jax.ShapeDtypeStruct((m, widths[0]), BF16),
            jax.ShapeDtypeStruct((m, widths[1]), BF16),
            jax.ShapeDtypeStruct((m, widths[2]), BF16),
            jax.ShapeDtypeStruct((m, n_lora), BF16),
        ],
        scratch_shapes=[pltpu.VMEM((tm, d), BF16)],
        compiler_params=_cparams(("parallel", "arbitrary"), 48),
        name="inproj",
    )(x2, g, w, w_lora)


def _gmlp_kernel(u_ref, v_ref, lnw_ref, lnb_ref, ws_ref, bst_ref, wpa_ref, gate_ref, o_ref, ya_ref, *, chunk):
    tm, da = v_ref.shape
    n_groups = ws_ref.shape[0]
    gw = da // n_groups
    v = v_ref[...].astype(F32)
    mu = jnp.mean(v, axis=-1, keepdims=True)
    d = v - mu
    var = jnp.mean(d * d, axis=-1, keepdims=True)
    vn = (d * lax.rsqrt(var + LN_EPS) * lnw_ref[...] + lnb_ref[...]).astype(BF16)
    row = lax.broadcasted_iota(jnp.int32, (chunk, chunk), 0)
    col = lax.broadcasted_iota(jnp.int32, (chunk, chunk), 1)
    causal = row >= col
    for g in range(n_groups):
        wm = jnp.where(causal, ws_ref[g], 0.0).astype(BF16)
        bias = bst_ref[:, g:g + 1]
        for c in range(tm // chunk):
            rs = slice(c * chunk, (c + 1) * chunk)
            cs = slice(g * gw, (g + 1) * gw)
            mixed = jnp.dot(wm, vn[rs, cs], preferred_element_type=F32) + bias
            ya_ref[rs, cs] = (u_ref[rs, cs].astype(F32) * mixed).astype(BF16)
    pa = jnp.dot(ya_ref[...], wpa_ref[...], preferred_element_type=F32)
    o_ref[...] = (pa * gate_ref[...].astype(F32)).astype(o_ref.dtype)


def _gmlp(uv, gates, lnw, lnb, ws, bst, wpa, *, tm):
    m = uv.shape[0]
    da = uv.shape[1] // 2
    dm = wpa.shape[1]
    chunk = ws.shape[1]
    return pl.pallas_call(
        functools.partial(_gmlp_kernel, chunk=chunk),
        grid=(m // tm,),
        in_specs=[
            pl.BlockSpec((tm, da), lambda i: (i, 0)),
            pl.BlockSpec((tm, da), lambda i: (i, 1)),
            pl.BlockSpec((1, da), lambda i: (0, 0)),
            pl.BlockSpec((1, da), lambda i: (0, 0)),
            pl.BlockSpec(ws.shape, lambda i: (0, 0, 0)),
            pl.BlockSpec(bst.shape, lambda i: (0, 0)),
            pl.BlockSpec(wpa.shape, lambda i: (0, 0), pipeline_mode=pl.Buffered(1)),
            pl.BlockSpec((tm, dm), lambda i: (i, 0)),
        ],
        out_specs=pl.BlockSpec((tm, dm), lambda i: (i, 0)),
        out_shape=jax.ShapeDtypeStruct((m, dm), BF16),
        scratch_shapes=[pltpu.VMEM((tm, da), BF16)],
        compiler_params=_cparams(("parallel",), 48),
        name="gmlp",
    )(uv, uv, lnw, lnb, ws, bst, wpa, gates)


def _head_sum(x, bd):
    xb = x.astype(BF16)
    outs = [jnp.dot(xb[:, j:j + MXU_WIDTH], bd, preferred_element_type=F32) for j in range(0, x.shape[1], MXU_WIDTH)]
    return jnp.concatenate(outs, axis=1)


def _prep_kernel(feat_ref, prev_ref, lora_ref, lprev_ref, mu_ref, mul_ref, w0_ref, a0_ref, kk_ref, ka_ref,
                 w01_ref, g2_ref, bd_ref, r_o, lw_o, k_o, v_o, a_o, b_o, g_o, *, seq, db, lora_wa):
    tm = feat_ref.shape[0]
    i = pl.program_id(0)
    first = (i * tm) % seq == 0
    rows = lax.broadcasted_iota(jnp.int32, (tm, 1), 0)

    def token_shift_mix(cur_ref, prv_ref, m_ref):
        cur = cur_ref[...].astype(F32)
        halo = prv_ref.shape[0]
        prev = jnp.where(first, 0.0, prv_ref[halo - 1:halo, :].astype(F32))
        shifted = jnp.where(rows == 0, prev, pltpu.roll(cur, 1, 0))
        return cur + (shifted - cur) * m_ref[...]

    fm = token_shift_mix(feat_ref, prev_ref, mu_ref)
    fl = token_shift_mix(lora_ref, lprev_ref, mul_ref)
    r = fm[:, 0:db]
    k = fm[:, db:2 * db]
    v = fm[:, 2 * db:3 * db]
    l0 = fl[:, :LANES]
    l1 = fl[:, LANES:]
    lane = lax.broadcasted_iota(jnp.int32, l0.shape, 1)
    x0 = jnp.where(lane < lora_wa, jnp.tanh(l0), l0)
    wa = _bdot(x0, w01_ref[...])
    g = _bdot(_sigmoid(l1), g2_ref[...])
    w_raw = w0_ref[...] + wa[:, :db]
    a = _sigmoid(a0_ref[...] + wa[:, db:])
    lw = -EXP_NEG_HALF * _sigmoid(w_raw)
    kkv = k * kk_ref[...]
    kkn = kkv * lax.rsqrt(jnp.maximum(_head_sum(kkv * kkv, bd_ref[...]), KK_NORM_FLOOR ** 2))
    kp = k * (1.0 + (a - 1.0) * ka_ref[...])
    lw_o[...] = lw
    for o_ref, val in ((r_o, r), (k_o, kp), (v_o, v), (a_o, -kkn), (b_o, kkn * a), (g_o, g)):
        o_ref[...] = val.astype(o_ref.dtype)


def _rwkv_prep(feat, lora, mu, mu_lora, w0, a0, kk, ka, w01, g2p, bd, *, seq, tm, lora_wa):
    m = feat.shape[0]
    nf = mu.shape[1]
    nl = mu_lora.shape[1]
    db = w0.shape[1]
    vec = lambda n: pl.BlockSpec((1, n), lambda i: (0, 0))
    full = lambda a: pl.BlockSpec(a.shape, lambda i: (0, 0))
    out_spec = pl.BlockSpec((tm, db), lambda i: (i, 0))
    assert feat.dtype == BF16 and lora.dtype == BF16
    halo = SUBLANES_BF16
    blocks_per_tile = tm // halo
    prev_row = lambda n: pl.BlockSpec((halo, n), lambda i: (jnp.maximum(i * blocks_per_tile - 1, 0), 0))
    return pl.pallas_call(
        functools.partial(_prep_kernel, seq=seq, db=db, lora_wa=lora_wa),
        grid=(m // tm,),
        in_specs=[
            pl.BlockSpec((tm, nf), lambda i: (i, 0)), prev_row(nf),
            pl.BlockSpec((tm, nl), lambda i: (i, 0)), prev_row(nl),
            vec(nf), vec(nl), vec(db), vec(db), vec(db), vec(db), full(w01), full(g2p), full(bd),
        ],
        out_specs=[out_spec] * 7,
        out_shape=[jax.ShapeDtypeStruct((m, db), F32 if i == 1 else BF16) for i in range(7)],
        compiler_params=_cparams(("parallel",), 48),
        name="rwkv_prep",
    )(feat, feat, lora, lora, mu, mu_lora, w0, a0, kk, ka, w01, g2p, bd)


def _mm(a, b, dims):
    return lax.dot_general(a.astype(BF16), b.astype(BF16), (dims, ((), ())), preferred_element_type=F32)


_NN = ((1,), (0,))
_NT = ((1,), (1,))
_TN = ((0,), (0,))


def _scan_kernel(r_ref, lw_ref, k_ref, v_ref, a_ref, b_ref, y_ref, st_ref, *, head, chunk):
    c = pl.program_id(1)
    C = chunk
    n = lw_ref.shape[2]
    pair = 2 * head
    assert pair == LANES and 2 * C == LANES
    pairs = range(n // pair)
    sls = [slice(p * pair, (p + 1) * pair) for p in pairs]

    @pl.when(c == 0)
    def _():
        st_ref[...] = jnp.zeros_like(st_ref)

    rc = lax.broadcasted_iota(jnp.int32, (C, C), 0)
    cc = lax.broadcasted_iota(jnp.int32, (C, C), 1)
    tri = (rc >= cc).astype(F32)
    lane = lax.broadcasted_iota(jnp.int32, (C, pair), 1)
    m0 = lane < head
    ri = lax.broadcasted_iota(jnp.int32, (pair, pair), 0)
    ci = lax.broadcasted_iota(jnp.int32, (pair, pair), 1)
    same = (ri >= C) == (ci >= C)
    strict = same & (ri > ci)
    incl = same & (ri >= ci)
    eye = ri == ci
    eye_f = eye.astype(F32)
    zero = jnp.zeros((pair, pair), F32)

    def stack(x):
        return jnp.concatenate([jnp.where(m0, x, 0.0), jnp.where(m0, 0.0, x)], axis=0)

    def one_chunk(rows, st0):
        lw = lw_ref[0, rows, :]
        L = lax.dot_general(tri, lw, (_NN, ((), ())), precision=lax.Precision.HIGHEST, preferred_element_type=F32)
        LC = L[C - 1:C, :]
        p_in = jnp.exp(L)
        p_ex = jnp.exp(L - lw)
        p_inv = jnp.exp(-L)
        p_end = jnp.exp(LC - L)
        p_c = jnp.exp(LC)
        r, k, v, a, b = (ref[0, rows, :].astype(F32) for ref in (r_ref, k_ref, v_ref, a_ref, b_ref))
        rt = r * p_in
        at = a * p_ex
        bt = b * p_inv
        kt = k * p_inv
        bh = b * p_end
        kh = k * p_end
        ats, rts, bts, kts, vs, bhs, khs = ([stack(x[:, sl]) for sl in sls] for x in (at, rt, bt, kt, v, bh, kh))
        amat = [_mm(jnp.concatenate([ats[p], rts[p]], 0), jnp.concatenate([bts[p], kts[p]], 0), _NT) for p in pairs]
        aab = [jnp.where(strict, amat[p][:pair, :pair], 0.0) for p in pairs]
        aak = [jnp.where(strict, amat[p][:pair, pair:], 0.0) for p in pairs]
        arb = [jnp.where(incl, amat[p][pair:, :pair], 0.0) for p in pairs]
        ark = [jnp.where(incl, amat[p][pair:, pair:], 0.0) for p in pairs]
        t_inv = [eye_f + aab[p] for p in pairs]
        aj = [_mm(aab[p], aab[p], _NN) for p in pairs]
        w1 = [_mm(aak[p], vs[p], _NN) for p in pairs]
        n_factors = C.bit_length() - 1
        for _ in range(n_factors - 2):
            res = [_mm(aj[p], jnp.concatenate([t_inv[p], aj[p]], 1), _NN) for p in pairs]
            t_inv = [t_inv[p] + res[p][:, :pair] for p in pairs]
            aj = [res[p][:, pair:] for p in pairs]
        t_inv = [t_inv[p] + _mm(aj[p], t_inv[p], _NN) for p in pairs]
        x = [_mm(t_inv[p], jnp.concatenate([ats[p], w1[p]], 1), _NN) for p in pairs]
        xv = [jnp.concatenate([x[p], jnp.concatenate([zero, vs[p]], 1)], 0) for p in pairs]
        z = [_mm(jnp.concatenate([arb[p], ark[p]], 1), xv[p], _NN) for p in pairs]
        gmat = [_mm(jnp.concatenate([bhs[p], khs[p]], 0), xv[p], _TN) for p in pairs]
        st1 = []
        for p in pairs:
            mt = jnp.where(eye, p_c[:, sls[p]], 0.0) + gmat[p][:, :pair]
            st1.append(_mm(mt, st0[p], _NN) + gmat[p][:, pair:])
        for p in pairs:
            rp_s = rts[p] + z[p][:, :pair]
            yv_s = z[p][:, pair:]
            rp = rp_s[:C] + rp_s[C:]
            yv = yv_s[:C] + yv_s[C:]
            y_ref[0, rows, sls[p]] = _mm(rp, st0[p], _NN) + yv
        return st1

    st = [st_ref[p] for p in pairs]
    for q in range(lw_ref.shape[1] // C):
        st = one_chunk(slice(q * C, (q + 1) * C), st)
    for p in pairs:
        st_ref[p] = st[p]


def _wkv_scan(r, lw, k, v, a, b, *, head):
    bsz, seq, n = r.shape
    rows = SCAN_CHUNK * SCAN_CHUNKS_PER_STEP
    spec = pl.BlockSpec((1, rows, n), lambda i, c: (i, c, 0))
    return pl.pallas_call(
        functools.partial(_scan_kernel, head=head, chunk=SCAN_CHUNK),
        grid=(bsz, seq // rows),
        in_specs=[spec] * 6,
        out_specs=spec,
        out_shape=jax.ShapeDtypeStruct((bsz, seq, n), F32),
        scratch_shapes=[pltpu.VMEM((n // (2 * head), 2 * head, 2 * head), F32)],
        compiler_params=_cparams(("arbitrary", "arbitrary"), 48),
        name="wkv_scan",
    )(r, lw, k, v, a, b)


def _post_kernel(y_ref, r_ref, k_ref, v_ref, g_ref, rk_ref, lnw_ref, lnb_ref, bd_ref, wpb_ref, gate_ref,
                 pa_ref, wout_ref, x_ref, n2_ref, x1_o, h2_o, *, head):
    bd = bd_ref[...]
    y = y_ref[...]
    inv = 1.0 / head
    mean = _head_sum(y, bd) * inv
    d = y - mean
    var = _head_sum(d * d, bd) * inv
    yn = d * lax.rsqrt(var + GN_EPS) * lnw_ref[...] + lnb_ref[...]
    r, k, v = (ref[...].astype(F32) for ref in (r_ref, k_ref, v_ref))
    bonus = _head_sum(r * k * rk_ref[...], bd) * v
    yb = ((yn + bonus) * g_ref[...].astype(F32)).astype(BF16)
    pb = jnp.dot(yb, wpb_ref[...], preferred_element_type=F32) * gate_ref[...].astype(F32)
    merged = (pa_ref[...].astype(F32) + pb).astype(BF16)
    x1 = x_ref[...] + jnp.dot(merged, wout_ref[...], preferred_element_type=F32)
    x1_o[...] = x1
    h2_o[...] = _rmsnorm(x1, n2_ref[...]).astype(h2_o.dtype)


def _rwkv_post(y, r, k, v, g, rk, lnw, lnb, bd, wpb, gates, pa, wout, x2, n2, *, head, tm):
    m, db = y.shape
    dm = x2.shape[1]
    row = lambda n, j=0: pl.BlockSpec((tm, n), lambda i: (i, j))
    vec = lambda n: pl.BlockSpec((1, n), lambda i: (0, 0))
    full = lambda a: pl.BlockSpec(a.shape, lambda i: (0, 0), pipeline_mode=pl.Buffered(1))
    return pl.pallas_call(
        functools.partial(_post_kernel, head=head),
        grid=(m // tm,),
        in_specs=[row(db), row(db), row(db), row(db), row(db), vec(db), vec(db), vec(db), full(bd), full(wpb),
                  row(dm, 1), row(dm), full(wout), row(dm), vec(dm)],
        out_specs=[row(dm), row(dm)],
        out_shape=[jax.ShapeDtypeStruct((m, dm), F32), jax.ShapeDtypeStruct((m, dm), BF16)],
        compiler_params=_cparams(("parallel",), 60),
        name="rwkv_post",
    )(y, r, k, v, g, rk, lnw, lnb, bd, wpb, gates, pa, wout, x2, n2)


def _ffn_kernel(h_ref, hp_ref, x1_ref, wg_ref, wv_ref, cwg_ref, cwv_ref, cbg_ref, cbv_ref, wd_ref, nf_ref,
                o_ref, hext_ref, ug_ref, uv_ref, acc_ref, *, seq, taps):
    tm = h_ref.shape[0]
    halo = hp_ref.shape[0]
    i = pl.program_id(0)
    j = pl.program_id(1)

    @pl.when(j == 0)
    def _():
        first = (i * tm) % seq == 0
        hext_ref[0:halo, :] = jnp.where(first, jnp.zeros_like(hp_ref), hp_ref[...])
        hext_ref[halo:, :] = h_ref[...]
        acc_ref[...] = jnp.zeros_like(acc_ref)

    hext = hext_ref[...]
    ug_ref[...] = jnp.dot(hext, wg_ref[...], preferred_element_type=F32)
    uv_ref[...] = jnp.dot(hext, wv_ref[...], preferred_element_type=F32)

    def conv(u_ref, cw_ref, cb_ref):
        out = cb_ref[...]
        for t in range(taps):
            off = halo - (taps - 1) + t
            out = out + cw_ref[t:t + 1, :] * u_ref[off:off + tm, :]
        return out

    act = (_gelu(conv(ug_ref, cwg_ref, cbg_ref)) * conv(uv_ref, cwv_ref, cbv_ref)).astype(BF16)
    acc_ref[...] += jnp.dot(act, wd_ref[...], preferred_element_type=F32)

    @pl.when(j == pl.num_programs(1) - 1)
    def _():
        o_ref[...] = _rmsnorm(x1_ref[...] + acc_ref[...], nf_ref[...])


def _conv_ffn(h2, x1, wup, cw, cb, wd, nf, *, seq, tm, tn):
    m, dm = h2.shape
    dff = wd.shape[0]
    nj = dff // tn
    taps = cw.shape[0]
    halo = SUBLANES_BF16
    blocks_per_tile = tm // halo
    return pl.pallas_call(
        functools.partial(_ffn_kernel, seq=seq, taps=taps),
        grid=(m // tm, nj),
        in_specs=[
            pl.BlockSpec((tm, dm), lambda i, j: (i, 0)),
            pl.BlockSpec((halo, dm), lambda i, j: (jnp.maximum(i * blocks_per_tile - 1, 0), 0)),
            pl.BlockSpec((tm, dm), lambda i, j: (i, 0)),
            pl.BlockSpec((dm, tn), lambda i, j: (0, j)),
            pl.BlockSpec((dm, tn), lambda i, j: (0, nj + j)),
            pl.BlockSpec((taps, tn), lambda i, j: (0, j)),
            pl.BlockSpec((taps, tn), lambda i, j: (0, nj + j)),
            pl.BlockSpec((1, tn), lambda i, j: (0, j)),
            pl.BlockSpec((1, tn), lambda i, j: (0, nj + j)),
            pl.BlockSpec((tn, dm), lambda i, j: (j, 0)),
            pl.BlockSpec((1, dm), lambda i, j: (0, 0)),
        ],
        out_specs=pl.BlockSpec((tm, dm), lambda i, j: (i, 0)),
        out_shape=jax.ShapeDtypeStruct((m, dm), F32),
        scratch_shapes=[
            pltpu.VMEM((tm + halo, dm), BF16),
            pltpu.VMEM((tm + halo, tn), F32),
            pltpu.VMEM((tm + halo, tn), F32),
            pltpu.VMEM((tm, dm), F32),
        ],
        compiler_params=_cparams(("parallel", "arbitrary"), 56),
        name="conv_ffn",
    )(h2, h2, x1, wup, wup, cw, cw, cb, cb, wd, nf)


def _pick_tile(n, want):
    t = min(want, n)
    while n % t:
        t //= 2
    return t


def _block_diag_ones(width, block):
    idx = jnp.arange(width) // block
    return (idx[:, None] == idx[None, :]).astype(BF16)


def _layer(x2, seq, p):
    m, dm = x2.shape
    da = p["gmlp_ln_w"].shape[0]
    db = p["rwkv_w0"].shape[0]
    n_heads, head = p["rwkv_rk"].shape
    lora_w = p["rwkv_w2"].shape[0]
    lora_a = p["rwkv_a2"].shape[0]
    lora_g = p["rwkv_g2"].shape[0]
    assert lora_w + lora_a == LANES and lora_g <= MXU_WIDTH
    d_b_in = 3 * db + lora_w + lora_a + lora_g
    w_in = p["w_in"]
    n_lora = LANES + MXU_WIDTH
    n_rkv = 3 * db
    tn_in = 4 * MXU_WIDTH
    assert n_rkv % tn_in == 0
    c0, c1, c2 = 2 * da, 2 * da + n_rkv, 2 * da + d_b_in
    w_all = jnp.concatenate([w_in[:, :c1].astype(BF16), w_in[:, c2:].astype(BF16)], axis=1)
    w_lora = jnp.pad(w_in[:, c1:c2].astype(BF16), ((0, 0), (0, n_lora - (c2 - c1))))
    uv, feat, gates, lora = _inproj(x2, p["norm1_g"].reshape(1, dm), w_all, w_lora,
                                    widths=(c0, n_rkv, 2 * dm), tm=_pick_tile(m, 1024), tn=tn_in)

    pa = _gmlp(uv, gates, p["gmlp_ln_w"].reshape(1, da), p["gmlp_ln_b"].reshape(1, da), p["gmlp_ws"],
               p["gmlp_bs"].T, p["w_proj_a"].astype(BF16), tm=_pick_tile(seq, 1024))

    mu = p["mu_b"][:n_rkv].reshape(1, n_rkv)
    mu_lora = jnp.pad(p["mu_b"][n_rkv:], (0, n_lora - (d_b_in - n_rkv))).reshape(1, n_lora)
    zeros = jnp.zeros((lora_w, db), F32)
    w01 = jnp.concatenate([jnp.concatenate([p["rwkv_w2"], zeros], 1),
                           jnp.concatenate([jnp.zeros((lora_a, db), F32), p["rwkv_a2"]], 1)], 0).astype(BF16)
    g2p = jnp.pad(p["rwkv_g2"], ((0, MXU_WIDTH - lora_g), (0, 0))).astype(BF16)
    bd = _block_diag_ones(MXU_WIDTH, head)
    vecb = lambda a: a.reshape(1, db)
    r, lw, k, v, av, bv, g = _rwkv_prep(
        feat, lora, mu, mu_lora, vecb(p["rwkv_w0"]), vecb(p["rwkv_a0"]), vecb(p["rwkv_kk"]), vecb(p["rwkv_ka"]),
        w01, g2p, bd, seq=seq, tm=_pick_tile(seq, 512), lora_wa=lora_w)
    bsz = m // seq
    to3 = lambda t: t.reshape(bsz, seq, db)
    y = _wkv_scan(to3(r), to3(lw), to3(k), to3(v), to3(av), to3(bv), head=head).reshape(m, db)
    x1, h2 = _rwkv_post(y, r, k, v, g, vecb(p["rwkv_rk"]), vecb(p["rwkv_ln_w"]), vecb(p["rwkv_ln_b"]), bd,
                        p["w_proj_b"].astype(BF16), gates, pa, p["w_out"].astype(BF16), x2,
                        p["norm2_g"].reshape(1, dm), head=head, tm=_pick_tile(seq, 512))
    return x1, h2


def kernel(x, norm1_g, w_in, mu_b, rwkv_w0, rwkv_w2, rwkv_a0, rwkv_a2, rwkv_g2, rwkv_kk, rwkv_ka, rwkv_rk, rwkv_ln_w, rwkv_ln_b, gmlp_ln_w, gmlp_ln_b, gmlp_ws, gmlp_bs, w_proj_a, w_proj_b, w_out, norm2_g, w_up, conv_w, conv_b, w_down, norm_f_g):
    bsz, seq, dm = x.shape
    depth = w_in.shape[0]
    assert depth == 1, "the fused FFN applies the final rmsnorm; one layer is supported"
    x2 = x.reshape(bsz * seq, dm)
    l = 0
    p = dict(norm1_g=norm1_g[l], w_in=w_in[l], mu_b=mu_b[l], rwkv_w0=rwkv_w0[l], rwkv_w2=rwkv_w2[l],
             rwkv_a0=rwkv_a0[l], rwkv_a2=rwkv_a2[l], rwkv_g2=rwkv_g2[l], rwkv_kk=rwkv_kk[l], rwkv_ka=rwkv_ka[l],
             rwkv_rk=rwkv_rk[l], rwkv_ln_w=rwkv_ln_w[l], rwkv_ln_b=rwkv_ln_b[l], gmlp_ln_w=gmlp_ln_w[l],
             gmlp_ln_b=gmlp_ln_b[l], gmlp_ws=gmlp_ws[l], gmlp_bs=gmlp_bs[l], w_proj_a=w_proj_a[l],
             w_proj_b=w_proj_b[l], w_out=w_out[l], norm2_g=norm2_g[l])
    x1, h2 = _layer(x2, seq, p)
    dff = w_down.shape[1]
    out = _conv_ffn(h2, x1, w_up[l].astype(BF16), conv_w[l], conv_b[l].reshape(1, -1), w_down[l].astype(BF16),
                    norm_f_g.reshape(1, dm), seq=seq, tm=_pick_tile(seq, 512), tn=_pick_tile(dff, 512))
    return out.reshape(bsz, seq, dm)
```
